```python
import jax, jax.numpy as jnp
from jax import lax
import numpy as np

D_MODEL = 1024
BATCH = 8
SEQ = 2048
DEPTH = 4
DEC_BATCH = 128
DEC_SEQ = 8
PAST_LEN = 2048
PAGE_SIZE = 128

N_EVEN = (DEPTH + 1) // 2
N_ODD = DEPTH // 2
NORM_EPS = 1e-6
A_HD = 64
A_W = D_MODEL // 2
A_HEADS = A_W // A_HD
W_LORA = 64
AAA_LORA = 64
MV_LORA = 32
GATE_LORA = 128
LNX_EPS = 64e-5
RWKV_PROJ = 3 * A_W + W_LORA + AAA_LORA + GATE_LORA
B_HD = 128
B_W = D_MODEL // 2
B_HEADS = B_W // B_HD
RET_CHUNK = 128
GN_EPS = 1e-5
RET_PROJ = 4 * B_W
EVEN_PROJ = RWKV_PROJ + RET_PROJ
C_HD = 64
C_HEADS = D_MODEL // C_HD
C_W = C_HEADS * C_HD
FOX_BLOCK = 128
ODD_PROJ = 3 * C_W + C_HEADS
N_EXPERTS = 32
TOP_K = 4
D_FF = D_MODEL
SWIGLU_LIMIT = 7.0
SWIGLU_ALPHA = 1.702
MOE_BLOCK = 128

kernel_name = 'rwkv7_retnet_fox_moe_adaln_step'

F32 = jnp.float32


def _rmsnorm(x, g):
    xf = x.astype(F32)
    y = xf * lax.rsqrt(jnp.mean(xf * xf, axis=-1, keepdims=True) + NORM_EPS)
    return (y * g.astype(F32)).astype(x.dtype)


def _modulate(x, g, shift, scale):
    return _rmsnorm(x, g) * (1.0 + scale) + shift


def _head_norm(x, g, b, eps):
    xf = x.astype(F32)
    xc = xf - jnp.mean(xf, axis=-1, keepdims=True)
    y = xc * lax.rsqrt(jnp.mean(xc * xc, axis=-1, keepdims=True) + eps)
    y = y * g.astype(F32).reshape(x.shape[-2:])
    if b is not None:
        y = y + b.astype(F32).reshape(x.shape[-2:])
    return y


def _rotate(x, pos):
    half = x.shape[-1] // 2
    inv = 1.0 / (10000.0 ** jnp.linspace(0.0, 1.0, half, dtype=F32))
    ang = pos[:, None] * inv[None, :]
    cos = jnp.cos(ang)[None, :, None, :]
    sin = jnp.sin(ang)[None, :, None, :]
    x1, x2 = x[..., :half], x[..., half:]
    return jnp.concatenate([x1 * cos - x2 * sin, x1 * sin + x2 * cos], axis=-1)


def _rwkv_scan(S0, r, decay, k, v, kk, a):
    xs = tuple(jnp.moveaxis(t, 1, 0) for t in (r, decay, k, v, kk, a))

    def step(S, inp):
        r_t, w_t, k_t, v_t, kk_t, a_t = inp
        sa = jnp.einsum('bhij,bhj->bhi', S, -kk_t)
        S = (S * w_t[:, :, None, :] + sa[..., None] * (kk_t * a_t)[:, :, None, :]
             + v_t[..., None] * k_t[:, :, None, :])
        return S, jnp.einsum('bhij,bhj->bhi', S, r_t)

    S, out = lax.scan(step, S0, xs)
    return S, jnp.moveaxis(out, 0, 1)


def _rwkv7(p, shift_prev, S0, v_first, lp, vres):
    Bn, T, _ = p.shape
    prev = jnp.concatenate([shift_prev[:, None, :].astype(p.dtype), p[:, :-1]], axis=1)
    z = p + (prev - p) * lp['mu']
    cuts = [A_W, 2 * A_W, 3 * A_W, 3 * A_W + W_LORA, 3 * A_W + W_LORA + AAA_LORA]
    r, k, v, wi, ai, gi = jnp.split(z, cuts, axis=-1)
    w_log = -jax.nn.softplus(-(lp['w0'] + jnp.tanh(wi) @ lp['w2'])) - 0.5
    decay = jnp.exp(-jnp.exp(w_log.astype(F32)))
    a = jax.nn.sigmoid(lp['a0'] + ai @ lp['a2'])
    g = jax.nn.sigmoid(gi) @ lp['g2']
    if v_first is None:
        v_first = v
    else:
        v0, v1, v2 = vres
        v = v + (v_first - v) * jax.nn.sigmoid(v0 + (v @ v1) @ v2)
    hs = lambda t: t.reshape(Bn, T, A_HEADS, A_HD).astype(F32)
    kk = hs(k * lp['k_k'])
    kk = kk * lax.rsqrt(jnp.maximum(jnp.sum(kk * kk, axis=-1, keepdims=True), 1e-24))
    k = k * (1.0 + (a - 1.0) * lp['k_a'])
    rh, kh, vh = hs(r), hs(k), hs(v)
    S, out = _rwkv_scan(S0.astype(F32), rh, decay.reshape(Bn, T, A_HEADS, A_HD), kh, vh, kk, hs(a))
    out = _head_norm(out, lp['ln_g'], lp['ln_b'], LNX_EPS)
    out = out + jnp.sum(rh * kh * lp['r_k'].astype(F32), axis=-1, keepdims=True) * vh
    out = out.reshape(Bn, T, A_W).astype(p.dtype) * g
    return out, S, p[:, -1], v_first


def _retention(pr, S0, pos0, gn_g):
    Bn, T, _ = pr.shape
    q, k, v, g = jnp.split(pr, 4, axis=-1)
    hs = lambda t: t.reshape(Bn, T, B_HEADS, B_HD).astype(F32)
    pos = pos0 + jnp.arange(T, dtype=F32)
    q = _rotate(hs(q), pos)
    k = _rotate(hs(k), pos) * (B_HD ** -0.5)
    v = hs(v)
    L = T if T <= RET_CHUNK else RET_CHUNK
    nc = T // L
    lg = jnp.log1p(-jnp.exp2(-5.0 - jnp.arange(B_HEADS, dtype=F32)))
    idx = jnp.arange(L, dtype=F32)
    diff = idx[:, None] - idx[None, :]
    intra = jnp.where(diff >= 0, jnp.exp(jnp.maximum(diff, 0.0)[None] * lg[:, None, None]), 0.0)
    q_dec = jnp.exp((idx[:, None] + 1.0) * lg[None, :])
    k_dec = jnp.exp((L - 1.0 - idx[:, None]) * lg[None, :])
    c_dec = jnp.exp(L * lg)
    chunks = lambda t: jnp.moveaxis(t.reshape(Bn, nc, L, B_HEADS, B_HD), 1, 0)

    def step(S, inp):
        qc, kc, vc = inp
        att = jnp.einsum('blhd,bmhd->bhlm', qc, kc) * intra
        o = (jnp.einsum('bhlm,bmhe->blhe', att, vc)
             + jnp.einsum('blhd,bhde->blhe', qc * q_dec[:, :, None], S))
        S = S * c_dec[:, None, None] + jnp.einsum('blhd,blhe->bhde', kc * k_dec[:, :, None], vc)
        return S, o

    S, o = lax.scan(step, S0.astype(F32), (chunks(q), chunks(k), chunks(v)))
    o = jnp.moveaxis(o, 0, 1).reshape(Bn, T, B_HEADS, B_HD)
    o = _head_norm(o, gn_g, None, GN_EPS).reshape(Bn, T, B_W).astype(pr.dtype)
    return jax.nn.silu(g) * o, S


def _fox_prompt(q, k, v, logf):
    Bn, T, H, d = q.shape
    scale = d ** -0.5
    c = jnp.cumsum(logf, axis=1).transpose(0, 2, 1)
    outs = []
    for b0 in range(0, T, FOX_BLOCK):
        b1 = min(b0 + FOX_BLOCK, T)
        s = jnp.einsum('bqhd,bkhd->bhqk', q[:, b0:b1], k[:, :b1]).astype(F32) * scale
        s = s + c[:, :, b0:b1, None] - c[:, :, None, :b1]
        causal = jnp.arange(b0, b1)[:, None] >= jnp.arange(b1)[None, :]
        s = jnp.where(causal, s, -jnp.inf)
        pr = jax.nn.softmax(s, axis=-1).astype(v.dtype)
        outs.append(jnp.einsum('bhqk,bkhd->bqhd', pr, v[:, :b1]))
    return jnp.concatenate(outs, axis=1)


def _fox_sample(q, k, v, logf, ck, cv, clf, page_table):
    Bn, T, H, d = q.shape
    P = page_table.shape[1] * ck.shape[1]
    k_past = ck[page_table].reshape(Bn, P, H, d)
    v_past = cv[page_table].reshape(Bn, P, H, d)
    c_past = jnp.cumsum(clf[page_table].reshape(Bn, P, H).astype(F32), axis=1)
    c_new = c_past[:, -1:, :] + jnp.cumsum(logf, axis=1)
    c_past = c_past.transpose(0, 2, 1)
    c_new = c_new.transpose(0, 2, 1)
    scale = d ** -0.5
    s_past = (jnp.einsum('bqhd,bkhd->bhqk', q, k_past).astype(F32) * scale
              + c_new[..., :, None] - c_past[..., None, :])
    s_new = (jnp.einsum('bqhd,bkhd->bhqk', q, k).astype(F32) * scale
             + c_new[..., :, None] - c_new[..., None, :])
    s_new = jnp.where(jnp.tril(jnp.ones((T, T), bool)), s_new, -jnp.inf)
    pr = jax.nn.softmax(jnp.concatenate([s_past, s_new], axis=-1), axis=-1).astype(v.dtype)
    return (jnp.einsum('bhqk,bkhd->bqhd', pr[..., :P], v_past)
            + jnp.einsum('bhqk,bkhd->bqhd', pr[..., P:], v))


def _moe(h, w_router, b_router, w_gu, b_gu, w_down, b_down):
    Bn, T, D = h.shape
    n = Bn * T
    nk = n * TOP_K
    x = h.reshape(n, D)
    logits = (x @ w_router + b_router).astype(F32)
    top_v, top_i = lax.top_k(logits, TOP_K)
    gates = jax.nn.softmax(top_v, axis=-1)
    e_flat = top_i.reshape(nk)
    order = jnp.argsort(e_flat).astype(jnp.int32)
    e_sorted = e_flat[order]
    counts = jnp.bincount(e_flat, length=N_EXPERTS)
    padded = (counts + MOE_BLOCK - 1) // MOE_BLOCK * MOE_BLOCK
    pad_end = jnp.cumsum(padded)
    start = jnp.cumsum(counts) - counts
    dest = ((pad_end - padded)[e_sorted] + jnp.arange(nk) - start[e_sorted]).astype(jnp.int32)
    n_blocks = -(-(nk + N_EXPERTS * (MOE_BLOCK - 1)) // MOE_BLOCK)
    slot_tok = jnp.full((n_blocks * MOE_BLOCK,), n, jnp.int32).at[dest].set(order // TOP_K)
    block_exp = jnp.minimum(jnp.searchsorted(pad_end, jnp.arange(n_blocks) * MOE_BLOCK, side='right'),
                            N_EXPERTS - 1)
    x_pad = jnp.concatenate([x, jnp.zeros((1, D), x.dtype)], axis=0)

    def expert_block(args):
        toks, e = args
        gu = x_pad[toks] @ w_gu[e] + b_gu[e]
        gate = jnp.minimum(gu[:, :D_FF], SWIGLU_LIMIT)
        up = jnp.clip(gu[:, D_FF:], -SWIGLU_LIMIT, SWIGLU_LIMIT)
        act = (up + 1.0) * gate * jax.nn.sigmoid(SWIGLU_ALPHA * gate)
        return act @ w_down[e] + b_down[e]

    y_slots = lax.map(expert_block, (slot_tok.reshape(n_blocks, MOE_BLOCK), block_exp)).reshape(-1, D)
    slot_of = jnp.zeros((nk,), jnp.int32).at[order].set(dest)
    y = jnp.einsum('nkd,nk->nd', y_slots[slot_of].reshape(n, TOP_K, D), gates.astype(x.dtype))
    return y.reshape(Bn, T, D)


def _trunk(x, c, pos0, rw_S, rw_shift, ret_S, fox_past, W):
    new = {'rw_S': [], 'rw_shift': [], 'ret_S': [], 'fox_k': [], 'fox_v': [], 'fox_lf': []}
    v_first = None
    for i in range(DEPTH):
        mod = jax.nn.silu(c) @ W['w_ada'][i] + W['b_ada'][i]
        sh_m, sc_m, g_m, sh_f, sc_f, g_f = jnp.split(mod[:, None, :], 6, axis=-1)
        h = _modulate(x, W['norm_mix_g'][i], sh_m, sc_m)
        if i % 2 == 0:
            e = i // 2
            proj = h @ W['w_in_even'][e]
            lp = {'mu': W['rw_mu'][e], 'w0': W['rw_w0'][e], 'w2': W['rw_w2'][e], 'a0': W['rw_a0'][e],
                  'a2': W['rw_a2'][e], 'g2': W['rw_g2'][e], 'k_k': W['rw_kk'][e], 'k_a': W['rw_ka'][e],
                  'r_k': W['rw_rk'][e], 'ln_g': W['rw_ln_g'][e], 'ln_b': W['rw_ln_b'][e]}
            vres = None if e == 0 else (W['rw_v0'][e - 1], W['rw_v1'][e - 1], W['rw_v2'][e - 1])
            o_a, S_a, shift_a, v_first = _rwkv7(proj[..., :RWKV_PROJ], rw_shift[e], rw_S[e], v_first, lp, vres)
            o_b, S_b = _retention(proj[..., RWKV_PROJ:], ret_S[e], pos0, W['ret_gn_g'][e])
            mix = jnp.concatenate([o_a, o_b], axis=-1) @ W['w_out_even'][e]
            new['rw_S'].append(S_a)
            new['rw_shift'].append(shift_a)
            new['ret_S'].append(S_b)
        else:
            j = i // 2
            proj = h @ W['w_in_odd'][j]
            Bn, T, _ = proj.shape
            q = proj[..., :C_W].reshape(Bn, T, C_HEADS, C_HD)
            k = proj[..., C_W:2 * C_W].reshape(Bn, T, C_HEADS, C_HD)
            v = proj[..., 2 * C_W:3 * C_W].reshape(Bn, T, C_HEADS, C_HD)
            logf = jax.nn.log_sigmoid((proj[..., 3 * C_W:] + W['b_forget'][j]).astype(F32))
            if fox_past is None:
                o = _fox_prompt(q, k, v, logf)
            else:
                ck, cv, clf, pt = fox_past
                o = _fox_sample(q, k, v, logf, ck[j], cv[j], clf[j], pt)
            mix = o.reshape(Bn, T, C_W) @ W['w_out_odd'][j]
            new['fox_k'].append(k)
            new['fox_v'].append(v)
            new['fox_lf'].append(logf)
        x = x + g_m * mix
        h = _modulate(x, W['norm_ffn_g'][i], sh_f, sc_f)
        x = x + g_f * _moe(h, W['w_router'][i], W['b_router'][i], W['w_gu'][i], W['b_gu'][i],
                           W['w_down'][i], W['b_down'][i])
    y = _rmsnorm(x, W['final_g'])
    return y, {name: jnp.stack(vals) for name, vals in new.items()}


def setup_inputs(seed: int = 0) -> dict:
    key = jax.random.key(seed)
    keys = iter(jax.random.split(key, 64))

    def nrm(shape, s):
        return jax.random.normal(next(keys), shape, F32) * s

    n_pages = PAST_LEN // PAGE_SIZE
    n_phys = (DEC_BATCH * n_pages * 5) // 4
    n_ev1 = N_EVEN - 1
    return {
        'x_prompt': nrm((BATCH, SEQ, D_MODEL), 1.0),
        'x_sample': nrm((DEC_BATCH, DEC_SEQ, D_MODEL), 1.0),
        'c_prompt': nrm((BATCH, D_MODEL), 1.0),
        'c_sample': nrm((DEC_BATCH, D_MODEL), 1.0),
        'state_rwkv': nrm((N_EVEN, DEC_BATCH, A_HEADS, A_HD, A_HD), 0.3),
        'state_rwkv_shift': nrm((N_EVEN, DEC_BATCH, RWKV_PROJ), 1.0),
        'state_retention': nrm((N_EVEN, DEC_BATCH, B_HEADS, B_HD, B_HD), 0.5),
        'cache_fox_k': nrm((N_ODD, n_phys, PAGE_SIZE, C_HEADS, C_HD), 1.0),
        'cache_fox_v': nrm((N_ODD, n_phys, PAGE_SIZE, C_HEADS, C_HD), 1.0),
        'cache_fox_logf': jax.nn.log_sigmoid(3.0 + nrm((N_ODD, n_phys, PAGE_SIZE, C_HEADS), 0.5)),
        'page_table': jax.random.permutation(next(keys), n_phys)[:DEC_BATCH * n_pages]
                      .reshape(DEC_BATCH, n_pages).astype(jnp.int32),
        'w_ada': nrm((DEPTH, D_MODEL, 6 * D_MODEL), 0.5 * D_MODEL ** -0.5),
        'b_ada': nrm((DEPTH, 6 * D_MODEL), 0.02),
        'norm_mix_g': 1.0 + nrm((DEPTH, D_MODEL), 0.02),
        'norm_ffn_g': 1.0 + nrm((DEPTH, D_MODEL), 0.02),
        'final_g': 1.0 + nrm((D_MODEL,), 0.02),
        'w_in_even': nrm((N_EVEN, D_MODEL, EVEN_PROJ), D_MODEL ** -0.5),
        'w_out_even': nrm((N_EVEN, A_W + B_W, D_MODEL), (A_W + B_W) ** -0.5),
        'rw_mu': jax.random.uniform(next(keys), (N_EVEN, RWKV_PROJ), F32),
        'rw_w0': nrm((N_EVEN, A_W), 0.5),
        'rw_w2': nrm((N_EVEN, W_LORA, A_W), W_LORA ** -0.5),
        'rw_a0': nrm((N_EVEN, A_W), 0.1),
        'rw_a2': nrm((N_EVEN, AAA_LORA, A_W), AAA_LORA ** -0.5),
        'rw_g2': nrm((N_EVEN, GATE_LORA, A_W), GATE_LORA ** -0.5),
        'rw_kk': 0.85 + nrm((N_EVEN, A_W), 0.05),
        'rw_ka': 1.0 + nrm((N_EVEN, A_W), 0.05),
        'rw_rk': nrm((N_EVEN, A_HEADS, A_HD), 0.1),
        'rw_ln_g': 1.0 + nrm((N_EVEN, A_W), 0.02),
        'rw_ln_b': nrm((N_EVEN, A_W), 0.02),
        'rw_v0': nrm((n_ev1, A_W), 0.1),
        'rw_v1': nrm((n_ev1, A_W, MV_LORA), A_W ** -0.5),
        'rw_v2': nrm((n_ev1, MV_LORA, A_W), MV_LORA ** -0.5),
        'ret_gn_g': 1.0 + nrm((N_EVEN, B_W), 0.02),
        'w_in_odd': nrm((N_ODD, D_MODEL, ODD_PROJ), D_MODEL ** -0.5),
        'b_forget': 3.0 + nrm((N_ODD, C_HEADS), 0.5),
        'w_out_odd': nrm((N_ODD, C_W, D_MODEL), C_W ** -0.5),
        'w_router': nrm((DEPTH, D_MODEL, N_EXPERTS), D_MODEL ** -0.5),
        'b_router': nrm((DEPTH, N_EXPERTS), 0.01),
        'w_gu': nrm((DEPTH, N_EXPERTS, D_MODEL, 2 * D_FF), D_MODEL ** -0.5),
        'b_gu': nrm((DEPTH, N_EXPERTS, 2 * D_FF), 0.02),
        'w_down': nrm((DEPTH, N_EXPERTS, D_FF, D_MODEL), D_FF ** -0.5),
        'b_down': nrm((DEPTH, N_EXPERTS, D_MODEL), 0.02),
    }


def reference(x_prompt, x_sample, c_prompt, c_sample, state_rwkv, state_rwkv_shift, state_retention,
              cache_fox_k, cache_fox_v, cache_fox_logf, page_table,
              w_ada, b_ada, norm_mix_g, norm_ffn_g, final_g,
              w_in_even, w_out_even, rw_mu, rw_w0, rw_w2, rw_a0, rw_a2, rw_g2, rw_kk, rw_ka, rw_rk,
              rw_ln_g, rw_ln_b, rw_v0, rw_v1, rw_v2, ret_gn_g,
              w_in_odd, b_forget, w_out_odd,
              w_router, b_router, w_gu, b_gu, w_down, b_down):
    W = dict(w_ada=w_ada, b_ada=b_ada, norm_mix_g=norm_mix_g, norm_ffn_g=norm_ffn_g, final_g=final_g,
             w_in_even=w_in_even, w_out_even=w_out_even, rw_mu=rw_mu, rw_w0=rw_w0, rw_w2=rw_w2,
             rw_a0=rw_a0, rw_a2=rw_a2, rw_g2=rw_g2, rw_kk=rw_kk, rw_ka=rw_ka, rw_rk=rw_rk,
             rw_ln_g=rw_ln_g, rw_ln_b=rw_ln_b, rw_v0=rw_v0, rw_v1=rw_v1, rw_v2=rw_v2, ret_gn_g=ret_gn_g,
             w_in_odd=w_in_odd, b_forget=b_forget, w_out_odd=w_out_odd,
             w_router=w_router, b_router=b_router, w_gu=w_gu, b_gu=b_gu, w_down=w_down, b_down=b_down)
    bp = x_prompt.shape[0]
    rw_S0 = jnp.zeros((N_EVEN, bp, A_HEADS, A_HD, A_HD), F32)
    rw_shift0 = jnp.zeros((N_EVEN, bp, RWKV_PROJ), x_prompt.dtype)
    ret_S0 = jnp.zeros((N_EVEN, bp, B_HEADS, B_HD, B_HD), F32)
    y_prompt, sp = _trunk(x_prompt, c_prompt, 0, rw_S0, rw_shift0, ret_S0, None, W)
    past_len = page_table.shape[1] * cache_fox_k.shape[2]
    y_sample, ss = _trunk(x_sample, c_sample, past_len, state_rwkv, state_rwkv_shift, state_retention,
                          (cache_fox_k, cache_fox_v, cache_fox_logf, page_table), W)
    return (y_prompt, y_sample,
            sp['rw_S'], ss['rw_S'], sp['rw_shift'], ss['rw_shift'], sp['ret_S'], ss['ret_S'],
            sp['fox_k'], ss['fox_k'], sp['fox_v'], ss['fox_v'], sp['fox_lf'], ss['fox_lf'])
```

```python
import functools
import math

import numpy as np
import jax
import jax.numpy as jnp
from jax import lax
from jax.experimental import pallas as pl
from jax.experimental.pallas import tpu as pltpu

F32 = jnp.float32
BF16 = jnp.bfloat16
HI = lax.Precision.HIGHEST

D = 1024
NORM_EPS = 1e-6
A_W = 512
A_HEADS = 8
A_HD = 64
RWKV_PROJ = 1792
LNX_EPS = 64e-5
B_HEADS = 4
B_HD = 128
GN_EPS = 1e-5
C_HEADS = 16
C_HD = 64
N_EXPERTS = 32
TOP_K = 4
SWIGLU_LIMIT = 7.0
SWIGLU_ALPHA = 1.702

V7X_VMEM_BYTES = 64 * 1024 * 1024
VMEM_LIMIT = V7X_VMEM_BYTES - 8 * 1024 * 1024
LANES = 128
NEG = -1e30

TM = 512
MOE_TM = 256
CMB_TM = 256
RW_TC = 256
RET_L = 256
FOX_TQ = 256


def _cp(*sem):
    return pltpu.CompilerParams(dimension_semantics=sem, vmem_limit_bytes=VMEM_LIMIT)


def _sigmoid(x):
    return 1.0 / (1.0 + jnp.exp(-x))


def _softplus(x):
    return jnp.maximum(x, 0.0) + jnp.log(1.0 + jnp.exp(-jnp.abs(x)))


def _bdot(a, b):
    return jnp.dot(a.astype(BF16), b.astype(BF16), preferred_element_type=F32)


def _hdot(a, b):
    return jnp.dot(a, b, precision=HI, preferred_element_type=F32)


def _nt_dot(a, b):
    return lax.dot_general(a, b, (((1,), (1,)), ((), ())), preferred_element_type=F32)


def _rms(x, g):
    return x * lax.rsqrt(jnp.mean(x * x, axis=-1, keepdims=True) + NORM_EPS) * g


def _ada_kernel(c_ref, w_ref, b_ref, o_ref):
    c = c_ref[...]
    o_ref[...] = _bdot(c * _sigmoid(c), w_ref[...]) + b_ref[...]


def _ada(c_all, w_ada, b_ada):
    nl = w_ada.shape[0]
    m = c_all.shape[0]
    return pl.pallas_call(
        _ada_kernel,
        grid=(nl, 6),
        in_specs=[pl.BlockSpec((m, D), lambda l, j: (0, 0)),
                  pl.BlockSpec((None, D, D), lambda l, j: (l, 0, j)),
                  pl.BlockSpec((None, None, 1, D), lambda l, j: (l, j, 0, 0))],
        out_specs=pl.BlockSpec((None, None, m, D), lambda l, j: (l, j, 0, 0)),
        out_shape=jax.ShapeDtypeStruct((nl, 6, m, D), F32),
        compiler_params=_cp("arbitrary", "arbitrary"),
        name="ada",
    )(c_all, w_ada, b_ada.reshape(nl, 6, 1, D))


def _modp_spec(which, tm, seq, nb):
    tps = seq // tm
    return pl.BlockSpec((None, None, 1, D), lambda i, *_: (which, jnp.minimum(i // tps, nb - 1), 0, 0))


def _mods_spec(which, tm, npt):
    return pl.BlockSpec((None, tm, D), lambda i, *_: (which, jnp.maximum(i - npt, 0), 0))


def _inproj_kernel(x_ref, g_ref, shp, scp, shs, scs, w_ref, *rest, npt, has_extra):
    if has_extra:
        we_ref, be_ref, o_ref, oe_ref, h_scr = rest
    else:
        o_ref, h_scr = rest
    i = pl.program_id(0)
    j = pl.program_id(1)

    @pl.when(j == 0)
    def _():
        y = _rms(x_ref[...], g_ref[...])
        isp = i < npt
        sh = jnp.where(isp, shp[...], shs[...])
        sc = jnp.where(isp, scp[...], scs[...])
        h = (y * (1.0 + sc) + sh).astype(BF16)
        h_scr[...] = h
        if has_extra:
            f = jnp.dot(h, we_ref[...], preferred_element_type=F32) + be_ref[...]
            oe_ref[...] = jnp.minimum(f, 0.0) - jnp.log(1.0 + jnp.exp(-jnp.abs(f)))

    o_ref[...] = jnp.dot(h_scr[...], w_ref[...], preferred_element_type=F32)


def _inproj(x, g, modp, mods, w, tn, seq, extra=None):
    n = x.shape[0]
    nj = w.shape[1] // tn
    n_s = mods.shape[1]
    npt = (n - n_s) // TM
    nb = modp.shape[1]
    has_extra = extra is not None
    in_specs = [pl.BlockSpec((TM, D), lambda i, j: (i, 0)),
                pl.BlockSpec((1, D), lambda i, j: (0, 0)),
                _modp_spec(0, TM, seq, nb), _modp_spec(1, TM, seq, nb),
                _mods_spec(0, TM, npt), _mods_spec(1, TM, npt),
                pl.BlockSpec((D, tn), lambda i, j: (0, j))]
    args = [x, g.reshape(1, D), modp, modp, mods, mods, w]
    out_specs = [pl.BlockSpec((None, TM, tn), lambda i, j: (j, i, 0))]
    out_shape = [jax.ShapeDtypeStruct((nj, n, tn), F32)]
    if has_extra:
        we, be = extra
        in_specs += [pl.BlockSpec((D, LANES), lambda i, j: (0, 0)),
                     pl.BlockSpec((1, LANES), lambda i, j: (0, 0))]
        args += [we, be]
        out_specs.append(pl.BlockSpec((TM, LANES), lambda i, j: (i, 0)))
        out_shape.append(jax.ShapeDtypeStruct((n, LANES), F32))
    return pl.pallas_call(
        functools.partial(_inproj_kernel, npt=npt, has_extra=has_extra),
        grid=(n // TM, nj),
        in_specs=in_specs, out_specs=out_specs, out_shape=out_shape,
        scratch_shapes=[pltpu.VMEM((TM, D), BF16)],
        compiler_params=_cp("arbitrary", "arbitrary"),
        name="inproj",
    )(*args)


def _outproj_kernel(*refs, nparts, npt):
    x_ref = refs[0]
    mix = refs[1:1 + nparts]
    ws = refs[1 + nparts:1 + 2 * nparts]
    (gp, gs, nfg, shp, scp, shs, scs, wr, br, xo, ho, go, io) = refs[1 + 2 * nparts:]
    i = pl.program_id(0)
    acc = _bdot(mix[0][...], ws[0][...])
    for m, w in zip(mix[1:], ws[1:]):
        acc = acc + _bdot(m[...], w[...])
    isp = i < npt
    xn = x_ref[...] + jnp.where(isp, gp[...], gs[...]) * acc
    xo[...] = xn
    y = _rms(xn, nfg[...])
    h = y * (1.0 + jnp.where(isp, scp[...], scs[...])) + jnp.where(isp, shp[...], shs[...])
    ho[...] = h
    logits = _hdot(h, wr[...]) + br[...]
    lane = lax.broadcasted_iota(jnp.int32, logits.shape, 1)
    l = logits
    vals, idxs = [], []
    for _ in range(TOP_K):
        m = jnp.max(l, axis=-1, keepdims=True)
        ix = jnp.min(jnp.where(l == m, lane, LANES), axis=-1, keepdims=True)
        vals.append(m)
        idxs.append(ix)
        l = jnp.where(lane == ix, -jnp.inf, l)
    es = [jnp.exp(v - vals[0]) for v in vals]
    den = es[0] + es[1] + es[2] + es[3]
    gates = jnp.zeros(logits.shape, F32)
    idxo = jnp.zeros(logits.shape, jnp.int32)
    for k in range(TOP_K):
        gates = jnp.where(lane == k, es[k] / den, gates)
        idxo = jnp.where(lane == k, idxs[k], idxo)
    go[...] = gates
    io[...] = idxo


def _outproj(x, mixes, ws, modp, mods, nfg, wr, br, seq):
    n = x.shape[0]
    n_s = mods.shape[1]
    npt = (n - n_s) // TM
    nb = modp.shape[1]
    nparts = len(mixes)
    row = lambda i: (i, 0)
    const = lambda i: (0, 0)
    in_specs = [pl.BlockSpec((TM, D), row)]
    in_specs += [pl.BlockSpec((TM, m.shape[1]), row) for m in mixes]
    in_specs += [pl.BlockSpec(w.shape, const) for w in ws]
    in_specs += [_modp_spec(2, TM, seq, nb), _mods_spec(2, TM, npt),
                 pl.BlockSpec((1, D), const),
                 _modp_spec(3, TM, seq, nb), _modp_spec(4, TM, seq, nb),
                 _mods_spec(3, TM, npt), _mods_spec(4, TM, npt),
                 pl.BlockSpec((D, LANES), const), pl.BlockSpec((1, LANES), const)]
    args = [x, *mixes, *ws, modp, mods, nfg.reshape(1, D), modp, modp, mods, mods, wr, br]
    return pl.pallas_call(
        functools.partial(_outproj_kernel, nparts=nparts, npt=npt),
        grid=(n // TM,),
        in_specs=in_specs,
        out_specs=[pl.BlockSpec((TM, D), row), pl.BlockSpec((TM, D), row),
                   pl.BlockSpec((TM, LANES), row), pl.BlockSpec((TM, LANES), row)],
        out_shape=[jax.ShapeDtypeStruct((n, D), F32), jax.ShapeDtypeStruct((n, D), F32),
                   jax.ShapeDtypeStruct((n, LANES), F32), jax.ShapeDtypeStruct((n, LANES), jnp.int32)],
        compiler_params=_cp("arbitrary"),
        name="outproj",
    )(*args)


def _moe_kernel(tok_ref, bexp_ref, nused_ref, h_hbm, gate_ref, wgu_ref, bgu_ref, wd_ref, bd_ref,
                o_ref, xbuf, wgu_s, wd_s, sem, *, tm):
    blk = pl.program_id(0)
    nu = nused_ref[0]

    def issue(b, slot):
        base = b * tm

        def body(r, c):
            tok = tok_ref[base + r]
            pltpu.make_async_copy(h_hbm.at[pl.ds(tok, 1)], xbuf.at[slot, pl.ds(r, 1)], sem.at[slot]).start()
            return c

        lax.fori_loop(0, tm, body, 0)

    @pl.when(blk == 0)
    def _():
        issue(0, 0)

    @pl.when(blk + 1 < nu)
    def _():
        issue(blk + 1, (blk + 1) % 2)

    @pl.when(blk < nu)
    def _():
        slot = blk % 2
        pltpu.make_async_copy(h_hbm.at[pl.ds(0, tm)], xbuf.at[slot], sem.at[slot]).wait()
        e = bexp_ref[blk]
        e_prev = bexp_ref[jnp.maximum(blk - 1, 0)]

        @pl.when((blk == 0) | (e != e_prev))
        def _():
            wgu_s[...] = wgu_ref[...].astype(BF16)
            wd_s[...] = wd_ref[...].astype(BF16)

        xb = xbuf[slot].astype(BF16)
        gu = jnp.dot(xb, wgu_s[...], preferred_element_type=F32) + bgu_ref[...]
        gate = jnp.minimum(gu[:, :D], SWIGLU_LIMIT)
        up = jnp.clip(gu[:, D:], -SWIGLU_LIMIT, SWIGLU_LIMIT)
        act = (up + 1.0) * gate * _sigmoid(SWIGLU_ALPHA * gate)
        y = jnp.dot(act.astype(BF16), wd_s[...], preferred_element_type=F32) + bd_ref[...]
        o_ref[...] = y * gate_ref[...]

    @pl.when(blk >= nu)
    def _():
        o_ref[...] = jnp.zeros(o_ref.shape, F32)


def _moe_experts(h, slot_tok, block_exp, nused, slot_gate, layer, w_gu, b_gu, w_down, b_down):
    tm = MOE_TM
    nb = block_exp.shape[0]
    nl = w_gu.shape[0]
    grid_spec = pltpu.PrefetchScalarGridSpec(
        num_scalar_prefetch=3,
        grid=(nb,),
        in_specs=[pl.BlockSpec(memory_space=pl.ANY),
                  pl.BlockSpec((tm, 1), lambda b, t, e, u: (b, 0)),
                  pl.BlockSpec((None, None, D, 2 * D), lambda b, t, e, u: (layer, e[b], 0, 0)),
                  pl.BlockSpec((None, None, 1, 2 * D), lambda b, t, e, u: (layer, e[b], 0, 0)),
                  pl.BlockSpec((None, None, D, D), lambda b, t, e, u: (layer, e[b], 0, 0)),
                  pl.BlockSpec((None, None, 1, D), lambda b, t, e, u: (layer, e[b], 0, 0))],
        out_specs=pl.BlockSpec((tm, D), lambda b, t, e, u: (b, 0)),
        scratch_shapes=[pltpu.VMEM((2, tm, D), F32),
                        pltpu.VMEM((D, 2 * D), BF16),
                        pltpu.VMEM((D, D), BF16),
                        pltpu.SemaphoreType.DMA((2,))])
    return pl.pallas_call(
        functools.partial(_moe_kernel, tm=tm),
        grid_spec=grid_spec,
        out_shape=jax.ShapeDtypeStruct((nb * tm, D), F32),
        compiler_params=_cp("arbitrary"),
        name="moe_experts",
    )(slot_tok, block_exp, nused, h, slot_gate.reshape(nb * tm, 1), w_gu,
      b_gu.reshape(nl, N_EXPERTS, 1, 2 * D), w_down, b_down.reshape(nl, N_EXPERTS, 1, D))


def _combine_kernel(sof_ref, y_hbm, x_ref, gp, gs, fg_ref, o_ref, ybuf, sem, *, tm, npt, final):
    i = pl.program_id(0)
    nt = pl.num_programs(0)

    def issue(t, slot):
        base = t * (tm * TOP_K)

        def body(r, c):
            for k in range(TOP_K):
                s = sof_ref[base + r * TOP_K + k]
                pltpu.make_async_copy(y_hbm.at[pl.ds(s, 1)], ybuf.at[slot, k, pl.ds(r, 1)], sem.at[slot]).start()
            return c

        lax.fori_loop(0, tm, body, 0)

    @pl.when(i == 0)
    def _():
        issue(0, 0)

    @pl.when(i + 1 < nt)
    def _():
        issue(i + 1, (i + 1) % 2)

    slot = i % 2
    for k in range(TOP_K):
        pltpu.make_async_copy(y_hbm.at[pl.ds(0, tm)], ybuf.at[slot, k], sem.at[slot]).wait()
    y = (ybuf[slot, 0] + ybuf[slot, 1]) + (ybuf[slot, 2] + ybuf[slot, 3])
    xn = x_ref[...] + jnp.where(i < npt, gp[...], gs[...]) * y
    if final:
        xn = _rms(xn, fg_ref[...])
    o_ref[...] = xn


def _combine(x, y_slots, slot_of, modp, mods, final_g, seq, final):
    tm = CMB_TM
    n = x.shape[0]
    n_s = mods.shape[1]
    npt = (n - n_s) // tm
    nb = modp.shape[1]
    grid_spec = pltpu.PrefetchScalarGridSpec(
        num_scalar_prefetch=1,
        grid=(n // tm,),
        in_specs=[pl.BlockSpec(memory_space=pl.ANY),
                  pl.BlockSpec((tm, D), lambda i, s: (i, 0)),
                  _modp_spec(5, tm, seq, nb), _mods_spec(5, tm, npt),
                  pl.BlockSpec((1, D), lambda i, s: (0, 0))],
        out_specs=pl.BlockSpec((tm, D), lambda i, s: (i, 0)),
        scratch_shapes=[pltpu.VMEM((2, TOP_K, tm, D), F32), pltpu.SemaphoreType.DMA((2,))])
    return pl.pallas_call(
        functools.partial(_combine_kernel, tm=tm, npt=npt, final=final),
        grid_spec=grid_spec,
        out_shape=jax.ShapeDtypeStruct((n, D), F32),
        compiler_params=_cp("arbitrary"),
        name="moe_combine",
    )(slot_of, y_slots, x, modp, mods, final_g.reshape(1, D))


def _moe_meta(idx4, gate4, tm):
    n = idx4.shape[0]
    nk = n * TOP_K
    nb = -(-(nk + N_EXPERTS * (tm - 1)) // tm)
    sel = (idx4[:, :, None] == jnp.arange(N_EXPERTS, dtype=jnp.int32)[None, None, :])
    dense = jnp.any(sel, axis=1).astype(jnp.int32)
    csum = jnp.cumsum(dense, axis=0)
    counts = csum[-1]
    pos = csum - dense
    padded = (counts + tm - 1) // tm * tm
    pad_end = jnp.cumsum(padded)
    pad_start = pad_end - padded
    dest = (pad_start[idx4] + jnp.take_along_axis(pos, idx4, axis=1)).astype(jnp.int32)
    flat = dest.reshape(nk)
    tok = jnp.repeat(jnp.arange(n, dtype=jnp.int32), TOP_K)
    slot_tok = jnp.zeros((nb * tm,), jnp.int32).at[flat].set(tok)
    slot_gate = jnp.zeros((nb * tm,), F32).at[flat].set(gate4.reshape(nk))
    block_exp = jnp.minimum(jnp.searchsorted(pad_end, jnp.arange(nb, dtype=jnp.int32) * tm, side='right'),
                            N_EXPERTS - 1).astype(jnp.int32)
    nused = (pad_end[-1] // tm).astype(jnp.int32).reshape(1)
    return slot_tok, slot_gate, block_exp, nused, flat


def _seg_consts():
    r = np.arange(256)
    mseg = (r[:, None] // A_HD == r[None, :] // A_HD).astype(np.float32)
    c = np.arange(A_W)
    eyet = (np.arange(A_HD)[:, None] == (c[None, :] % A_HD)).astype(np.float32)
    return jnp.asarray(mseg, BF16), jnp.asarray(eyet, F32)


def _split(x):
    hi = x.astype(BF16)
    lo = (x - hi.astype(F32)).astype(BF16)
    return hi, lo


def _segsum(x, mseg):
    hi, lo = _split(x)
    halves = []
    for hf in range(2):
        sl = slice(hf * 256, (hf + 1) * 256)
        halves.append(jnp.dot(hi[:, sl], mseg, preferred_element_type=F32)
                      + jnp.dot(lo[:, sl], mseg, preferred_element_type=F32))
    return jnp.concatenate(halves, axis=1)


def _rwkv_kernel(*refs, nseq, tc, has_vres):
    it = iter(refs)
    p_ref, st_ref, s0_ref = next(it), next(it), next(it)
    vf_ref = next(it) if has_vres else None
    mu, w0, w2, a0, a2, g2, kkp, kap, rk, lng, lnb = (next(it) for _ in range(11))
    if has_vres:
        v0, v1, v2 = next(it), next(it), next(it)
    mseg_ref, eyet_ref = next(it), next(it)
    o_ref, sout_ref, shout_ref = next(it), next(it), next(it)
    vfo_ref = None if has_vres else next(it)
    w_s, k_s, v_s, nkk_s, b_s, r_s, out_s, st_s, prev_s = (next(it) for _ in range(9))

    c = pl.program_id(1)
    nch = pl.num_programs(1)
    rows = nseq * tc
    mseg = mseg_ref[...]
    eyet = eyet_ref[...]

    @pl.when(c == 0)
    def _():
        prev_s[...] = st_ref[...]
        for s in range(nseq):
            for h in range(A_HEADS):
                st_s[s, :, h * A_HD:(h + 1) * A_HD] = s0_ref[s, h]

    p = p_ref[:, :RWKV_PROJ]
    rolled = pltpu.roll(p, 1, 0)
    ridx = lax.broadcasted_iota(jnp.int32, (rows, 1), 0)
    if nseq == 1:
        prev = jnp.where(ridx == 0, prev_s[...], rolled)
    else:
        sidx = lax.broadcasted_iota(jnp.int32, (rows, nseq), 1)
        onehot = (lax.broadcasted_iota(jnp.int32, (rows, nseq), 0) == sidx * tc).astype(F32)
        prev = jnp.where(ridx % tc == 0, _hdot(onehot, prev_s[...]), rolled)
    z = p + (prev - p) * mu[...]
    r = z[:, 0:512]
    k = z[:, 512:1024]
    v = z[:, 1024:1536]
    wi = z[:, 1536:1600]
    ai = z[:, 1600:1664]
    gi = z[:, 1664:1792]
    w_log = -_softplus(-(w0[...] + _hdot(jnp.tanh(wi), w2[...]))) - 0.5
    w = jnp.exp(-jnp.exp(w_log))
    a = _sigmoid(a0[...] + _hdot(ai, a2[...]))
    g = _bdot(_sigmoid(gi), g2[...])
    if has_vres:
        v = v + (vf_ref[...] - v) * _sigmoid(v0[...] + _bdot(_bdot(v, v1[...]), v2[...]))
    else:
        vfo_ref[...] = v
    kk = k * kkp[...]
    kk = kk * lax.rsqrt(jnp.maximum(_segsum(kk * kk, mseg), 1e-24))
    k = k * (1.0 + (a - 1.0) * kap[...])
    w_s[...] = w
    k_s[...] = k
    v_s[...] = v
    nkk_s[...] = -kk
    b_s[...] = kk * a
    r_s[...] = r
    bonus = _segsum(r * k * rk[...], mseg) * v

    def halves(x):
        return [x[:, 0:256], x[:, 256:512]]

    def token(t, carry):
        lhs = []
        for s in range(nseq):
            row = s * tc + t
            st = st_s[s]
            x1h, x1l = _split(st * nkk_s[pl.ds(row, 1), :])
            dv = (eyet * v_s[pl.ds(row, 1), :]).astype(BF16)
            lhs += halves(x1h) + halves(x1l) + halves(dv)
        res = jnp.dot(jnp.concatenate(lhs, axis=0), mseg, preferred_element_type=F32)
        lhs2 = []
        for s in range(nseq):
            row = s * tc + t
            o = s * 6 * A_HD
            pc = [res[o + q * A_HD:o + (q + 1) * A_HD] for q in range(6)]
            sa = jnp.concatenate([pc[0] + pc[2], pc[1] + pc[3]], axis=1)
            v2 = jnp.concatenate([pc[4], pc[5]], axis=1)
            st = (st_s[s] * w_s[pl.ds(row, 1), :] + sa * b_s[pl.ds(row, 1), :]
                  + v2 * k_s[pl.ds(row, 1), :])
            st_s[s] = st
            lhs2 += halves((st * r_s[pl.ds(row, 1), :]).astype(BF16))
        res2 = jnp.dot(jnp.concatenate(lhs2, axis=0), mseg, preferred_element_type=F32)
        for s in range(nseq):
            row = s * tc + t
            o = s * 2 * A_HD
            o2 = jnp.concatenate([res2[o:o + A_HD], res2[o + A_HD:o + 2 * A_HD]], axis=1)
            out_s[pl.ds(row, 1), :] = jnp.sum(o2 * eyet, axis=0, keepdims=True)
        return carry

    lax.fori_loop(0, tc, token, 0)

    out = out_s[...]
    xc = out - _segsum(out, mseg) * (1.0 / A_HD)
    var = _segsum(xc * xc, mseg) * (1.0 / A_HD)
    y = xc * lax.rsqrt(var + LNX_EPS) * lng[...] + lnb[...]
    o_ref[...] = (y + bonus) * g

    for s in range(nseq):
        prev_s[s:s + 1, :] = p[s * tc + tc - 1:s * tc + tc, :]

    @pl.when(c == nch - 1)
    def _():
        shout_ref[...] = prev_s[...]
        for s in range(nseq):
            for h in range(A_HEADS):
                sout_ref[s, h] = st_s[s, :, h * A_HD:(h + 1) * A_HD]


def _rwkv(proj3, row0, nb, seq, nseq, tc, st_shift, s0, vfirst, lp, vres, base):
    n = proj3.shape[1]
    nch = seq // tc
    rows = nseq * tc
    rb0 = row0 // rows
    has_vres = vres is not None
    rowmap = lambda b, c: (rb0 + b * nch + c, 0)
    vec = lambda a: a.reshape(1, -1)
    c2 = lambda b, c: (0, 0)
    mseg, eyet = _seg_consts()
    in_specs = [pl.BlockSpec((None, rows, 2048), lambda b, c: (0, rb0 + b * nch + c, 0)),
                pl.BlockSpec((None, nseq, RWKV_PROJ), lambda b, c: (b, 0, 0)),
                pl.BlockSpec((nseq, A_HEADS, A_HD, A_HD), lambda b, c: (b, 0, 0, 0))]
    args = [proj3, st_shift.reshape(nb // nseq, nseq, RWKV_PROJ), s0]
    if has_vres:
        in_specs.append(pl.BlockSpec((rows, A_W), rowmap))
        args.append(vfirst)
    small = [vec(lp['mu']), vec(lp['w0']), lp['w2'], vec(lp['a0']), lp['a2'], lp['g2'].astype(BF16),
             vec(lp['k_k']), vec(lp['k_a']), vec(lp['r_k']), vec(lp['ln_g']), vec(lp['ln_b'])]
    if has_vres:
        small += [vec(vres[0]), vres[1].astype(BF16), vres[2].astype(BF16)]
    small += [mseg, eyet]
    in_specs += [pl.BlockSpec(a.shape, c2) for a in small]
    args += small
    out_specs = [pl.BlockSpec((rows, A_W), rowmap),
                 pl.BlockSpec((nseq, A_HEADS, A_HD, A_HD), lambda b, c: (b, 0, 0, 0)),
                 pl.BlockSpec((None, nseq, RWKV_PROJ), lambda b, c: (b, 0, 0))]
    out_shape = [jax.ShapeDtypeStruct((n, A_W), F32),
                 jax.ShapeDtypeStruct((nb, A_HEADS, A_HD, A_HD), F32),
                 jax.ShapeDtypeStruct((nb // nseq, nseq, RWKV_PROJ), F32)]
    aliases = {}
    if not has_vres:
        out_specs.append(pl.BlockSpec((rows, A_W), rowmap))
        out_shape.append(jax.ShapeDtypeStruct((n, A_W), F32))
    if base is not None:
        in_specs.append(pl.BlockSpec(memory_space=pl.ANY))
        args.append(base[0])
        aliases[len(args) - 1] = 0
        if not has_vres:
            in_specs.append(pl.BlockSpec(memory_space=pl.ANY))
            args.append(base[1])
            aliases[len(args) - 1] = 3
    scratch = [pltpu.VMEM((rows, A_W), F32) for _ in range(7)]
    scratch += [pltpu.VMEM((nseq, A_HD, A_W), F32), pltpu.VMEM((nseq, RWKV_PROJ), F32)]
    kern = functools.partial(_rwkv_kernel, nseq=nseq, tc=tc, has_vres=has_vres)
    if base is not None:
        nextra = len(aliases)
        inner = kern

        def kern(*refs):
            nin = len(in_specs)
            keep = refs[:nin - nextra] + refs[nin:]
            return inner(*keep)

    outs = pl.pallas_call(
        kern,
        grid=(nb // nseq, nch),
        in_specs=in_specs, out_specs=out_specs, out_shape=out_shape,
        scratch_shapes=scratch,
        input_output_aliases=aliases,
        compiler_params=_cp("arbitrary", "arbitrary"),
        name="rwkv",
    )(*args)
    o, s_out, sh_out = outs[0], outs[1], outs[2].reshape(nb, RWKV_PROJ)
    vf_out = vfirst if has_vres else outs[3]
    return o, s_out, sh_out, vf_out


def _ret_kernel(pr_ref, s0_ref, cos_ref, sin_ref, gn_ref, *rest, nseq, L, has_base):
    if has_base:
        _, o_ref, sout_ref, st_s = rest
    else:
        o_ref, sout_ref, st_s = rest
    c = pl.program_id(1)
    nch = pl.num_programs(1)

    @pl.when(c == 0)
    def _():
        st_s[...] = s0_ref[...]

    cos = cos_ref[...]
    sin = sin_ref[...]
    ii = lax.broadcasted_iota(jnp.int32, (L, L), 0)
    jj = lax.broadcasted_iota(jnp.int32, (L, L), 1)
    dif = (ii - jj).astype(F32)
    ri = lax.broadcasted_iota(jnp.int32, (L, 1), 0).astype(F32)

    def rot(x):
        return x * cos + pltpu.roll(x, B_HD // 2, 1) * sin

    for h in range(B_HEADS):
        lg = math.log1p(-2.0 ** (-5.0 - h))
        intra = jnp.where(dif >= 0, jnp.exp(jnp.maximum(dif, 0.0) * lg), 0.0)
        q_dec = jnp.exp((ri + 1.0) * lg)
        k_dec = jnp.exp((L - 1.0 - ri) * lg)
        c_dec = math.exp(L * lg)
        gn = gn_ref[:, h * B_HD:(h + 1) * B_HD]
        for s in range(nseq):
            rs = slice(s * L, (s + 1) * L)
            q = rot(pr_ref[rs, h * B_HD:(h + 1) * B_HD])
            k = rot(pr_ref[rs, 512 + h * B_HD:512 + (h + 1) * B_HD]) * (B_HD ** -0.5)
            v = pr_ref[rs, 1024 + h * B_HD:1024 + (h + 1) * B_HD]
            g = pr_ref[rs, 1536 + h * B_HD:1536 + (h + 1) * B_HD]
            st = st_s[s, h]
            vb = v.astype(BF16)
            att = _nt_dot(q.astype(BF16), k.astype(BF16)) * intra
            o = (jnp.dot(att.astype(BF16), vb, preferred_element_type=F32)
                 + _bdot(q * q_dec, st))
            st_s[s, h] = st * c_dec + lax.dot_general(
                (k * k_dec).astype(BF16), vb, (((0,), (0,)), ((), ())), preferred_element_type=F32)
            oc = o - jnp.mean(o, axis=-1, keepdims=True)
            y = oc * lax.rsqrt(jnp.mean(oc * oc, axis=-1, keepdims=True) + GN_EPS) * gn
            o_ref[rs, h * B_HD:(h + 1) * B_HD] = g * _sigmoid(g) * y

    @pl.when(c == nch - 1)
    def _():
        sout_ref[...] = st_s[...]


def _rot_tables(pos0, t):
    half = B_HD // 2
    inv = 1.0 / (10000.0 ** jnp.linspace(0.0, 1.0, half, dtype=F32))
    ang = (pos0 + jnp.arange(t, dtype=F32))[:, None] * inv[None, :]
    cos, sin = jnp.cos(ang), jnp.sin(ang)
    return jnp.concatenate([cos, cos], axis=1), jnp.concatenate([-sin, sin], axis=1)


def _retention(proj3, row0, nb, seq, nseq, L, pos0, s0, gn_g, base):
    n = proj3.shape[1]
    nch = seq // L
    rows = nseq * L
    rb0 = row0 // rows
    cos, sin = _rot_tables(float(pos0), seq)
    rowmap = lambda b, c: (rb0 + b * nch + c, 0)
    in_specs = [pl.BlockSpec((None, rows, 2048), lambda b, c: (1, rb0 + b * nch + c, 0)),
                pl.BlockSpec((nseq, B_HEADS, B_HD, B_HD), lambda b, c: (b, 0, 0, 0)),
                pl.BlockSpec((L, B_HD), lambda b, c: (c, 0)),
                pl.BlockSpec((L, B_HD), lambda b, c: (c, 0)),
                pl.BlockSpec((1, 512), lambda b, c: (0, 0))]
    args = [proj3, s0, cos, sin, gn_g.reshape(1, 512)]
    aliases = {}
    if base is not None:
        in_specs.append(pl.BlockSpec(memory_space=pl.ANY))
        args.append(base)
        aliases[len(args) - 1] = 0
    o, s_out = pl.pallas_call(
        functools.partial(_ret_kernel, nseq=nseq, L=L, has_base=base is not None),
        grid=(nb // nseq, nch),
        in_specs=in_specs,
        out_specs=[pl.BlockSpec((rows, 512), rowmap),
                   pl.BlockSpec((nseq, B_HEADS, B_HD, B_HD), lambda b, c: (b, 0, 0, 0))],
        out_shape=[jax.ShapeDtypeStruct((n, 512), F32),
                   jax.ShapeDtypeStruct((nb, B_HEADS, B_HD, B_HD), F32)],
        scratch_shapes=[pltpu.VMEM((nseq, B_HEADS, B_HD, B_HD), F32)],
        input_output_aliases=aliases,
        compiler_params=_cp("arbitrary", "arbitrary"),
        name="retention",
    )(*args)
    return o, s_out


def _tri(n):
    return (lax.broadcasted_iota(jnp.int32, (n, n), 0) >= lax.broadcasted_iota(jnp.int32, (n, n), 1)).astype(F32)


def _cumsum_kernel(lf_ref, o_ref):
    t = lf_ref.shape[0]
    tri = _tri(LANES)
    carry = jnp.zeros((1, LANES), F32)
    for b in range(t // LANES):
        rs = slice(b * LANES, (b + 1) * LANES)
        cb = _hdot(tri, lf_ref[rs, :]) + carry
        o_ref[rs, :] = cb
        carry = cb[LANES - 1:LANES, :]


def _cumsum_prompt(logf, nb, seq):
    return pl.pallas_call(
        _cumsum_kernel,
        grid=(nb,),
        in_specs=[pl.BlockSpec((seq, LANES), lambda b: (b, 0))],
        out_specs=pl.BlockSpec((seq, LANES), lambda b: (b, 0)),
        out_shape=jax.ShapeDtypeStruct((nb * seq, LANES), F32),
        compiler_params=_cp("arbitrary"),
        name="fox_cumsum",
    )(logf)


def _foxp_kernel(q_ref, k_ref, v_ref, cq_ref, ck_ref, o_ref, *, tq):
    hp = pl.program_id(1)
    qi = pl.program_id(2)
    lane = lax.broadcasted_iota(jnp.int32, (tq, LANES), 1)
    rowi = lax.broadcasted_iota(jnp.int32, (tq, tq), 0) + qi * tq
    coli = lax.broadcasted_iota(jnp.int32, (tq, tq), 1)
    outs = []
    for hh in range(2):
        h = hp * 2 + hh
        hs = slice(hh * C_HD, (hh + 1) * C_HD)
        q = (q_ref[:, hs] * (C_HD ** -0.5)).astype(BF16)
        cq = jnp.sum(jnp.where(lane == h, cq_ref[...], 0.0), axis=-1, keepdims=True)

        def body(kj, carry):
            m, l, acc = carry
            off = pl.multiple_of(kj * tq, tq)
            k = k_ref[pl.ds(off, tq), hs].astype(BF16)
            v = v_ref[pl.ds(off, tq), hs].astype(BF16)
            s = _nt_dot(q, k) + cq - ck_ref[pl.ds(h, 1), pl.ds(off, tq)]
            s = jnp.where(rowi >= coli + off, s, NEG)
            m_new = jnp.maximum(m, jnp.max(s, axis=-1, keepdims=True))
            alpha = jnp.exp(m - m_new)
            p = jnp.exp(s - m_new)
            l = alpha * l + jnp.sum(p, axis=-1, keepdims=True)
            acc = alpha * acc + jnp.dot(p.astype(BF16), v, preferred_element_type=F32)
            return m_new, l, acc

        init = (jnp.full((tq, 1), NEG, F32), jnp.zeros((tq, 1), F32), jnp.zeros((tq, C_HD), F32))
        m, l, acc = lax.fori_loop(0, qi + 1, body, init)
        outs.append(acc / l)
    o_ref[...] = jnp.concatenate(outs, axis=1)


def _fox_prompt(qkv3, c, ct, nb, seq):
    n = qkv3.shape[1]
    tq = FOX_TQ
    nq = seq // tq
    return pl.pallas_call(
        functools.partial(_foxp_kernel, tq=tq),
        grid=(nb, C_HEADS // 2, nq),
        in_specs=[pl.BlockSpec((None, tq, LANES), lambda b, hp, qi: (0, b * nq + qi, hp)),
                  pl.BlockSpec((None, seq, LANES), lambda b, hp, qi: (1, b, hp)),
                  pl.BlockSpec((None, seq, LANES), lambda b, hp, qi: (2, b, hp)),
                  pl.BlockSpec((tq, LANES), lambda b, hp, qi: (b * nq + qi, 0)),
                  pl.BlockSpec((None, C_HEADS, seq), lambda b, hp, qi: (b, 0, 0))],
        out_specs=pl.BlockSpec((tq, LANES), lambda b, hp, qi: (b * nq + qi, hp)),
        out_shape=jax.ShapeDtypeStruct((n, D), F32),
        compiler_params=_cp("arbitrary", "arbitrary", "arbitrary"),
        name="fox_prompt",
    )(qkv3, qkv3, qkv3, c, ct)


def _foxs_kernel(pt_ref, q_ref, kn_ref, vn_ref, lfn_ref, kt_ref, vt_ref, clf_ref, base_ref,
                 o_ref, qbd, cnq_s, m_s, l_s, acc_s, carry, *, t_new):
    g = pl.program_id(1)
    nrow = C_HEADS * t_new
    page = clf_ref.shape[1]
    rowh = lax.broadcasted_iota(jnp.int32, (nrow, D), 0) // t_new
    bmask = (lax.broadcasted_iota(jnp.int32, (nrow, D), 1) // C_HD) == rowh
    ex = (lax.broadcasted_iota(jnp.int32, (nrow, C_HEADS), 1)
          == lax.broadcasted_iota(jnp.int32, (nrow, C_HEADS), 0) // t_new).astype(F32)

    def update(s, pv):
        m = m_s[...]
        m_new = jnp.maximum(m, jnp.max(s, axis=-1, keepdims=True))
        alpha = jnp.exp(m - m_new)
        p = jnp.exp(s - m_new)
        l_s[...] = alpha * l_s[...] + jnp.sum(p, axis=-1, keepdims=True)
        acc_s[...] = alpha * acc_s[...] + pv(p)
        m_s[...] = m_new

    @pl.when(g == 0)
    def _():
        q = q_ref[...] * (C_HD ** -0.5)
        qbd[...] = jnp.where(bmask, jnp.concatenate([q] * C_HEADS, axis=0), 0.0).astype(BF16)
        m_s[...] = jnp.full(m_s.shape, NEG, F32)
        l_s[...] = jnp.zeros(l_s.shape, F32)
        acc_s[...] = jnp.zeros(acc_s.shape, F32)
        carry[...] = jnp.zeros(carry.shape, F32)
        tle = (lax.broadcasted_iota(jnp.int32, (t_new, t_new), 0)
               <= lax.broadcasted_iota(jnp.int32, (t_new, t_new), 1)).astype(F32)
        cnt = lax.dot_general(lfn_ref[:, :C_HEADS], tle, (((0,), (0,)), ((), ())),
                              precision=HI, preferred_element_type=F32)
        gk = _hdot(ex, cnt)
        colt = lax.broadcasted_iota(jnp.int32, (nrow, t_new), 1)
        qrow = lax.broadcasted_iota(jnp.int32, (nrow, t_new), 0) % t_new
        cnq = jnp.sum(jnp.where(colt == qrow, gk, 0.0), axis=-1, keepdims=True)
        cnq_s[...] = cnq
        s2 = _nt_dot(qbd[...], kn_ref[...].astype(BF16)) + cnq - gk
        update(jnp.where(colt <= qrow, s2, NEG),
               lambda p: jnp.dot(p, vn_ref[...], preferred_element_type=F32))

    lf = clf_ref[...]
    later = (lax.broadcasted_iota(jnp.int32, (page, page), 0)
             > lax.broadcasted_iota(jnp.int32, (page, page), 1)).astype(F32)
    dsuf = _hdot(lf, later) + carry[...]
    carry[...] = dsuf[:, 0:1] + lf[:, 0:1]
    s = (jnp.dot(qbd[...], kt_ref[...].astype(BF16), preferred_element_type=F32)
         + _hdot(ex, dsuf) + cnq_s[...])
    update(s, lambda p: _nt_dot(p.astype(BF16), vt_ref[...].astype(BF16)))

    @pl.when(g == pl.num_programs(1) - 1)
    def _():
        o = jnp.where(bmask, acc_s[...] / l_s[...], 0.0)
        o_ref[...] = jnp.sum(o.reshape(C_HEADS, t_new, D), axis=0)


def _fox_sample(qkv3, logf, row0, nb, t_new, page_table, layer, kt, vt, clft, base):
    n = qkv3.shape[1]
    npg = page_table.shape[1]
    page = kt.shape[3]
    nrow = C_HEADS * t_new
    rb0 = row0 // t_new
    pidx = lambda b, g, pt: (layer, pt[b * npg + (npg - 1 - g)], 0, 0)
    grid_spec = pltpu.PrefetchScalarGridSpec(
        num_scalar_prefetch=1,
        grid=(nb, npg),
        in_specs=[pl.BlockSpec((None, t_new, D), lambda b, g, pt: (0, rb0 + b, 0)),
                  pl.BlockSpec((None, t_new, D), lambda b, g, pt: (1, rb0 + b, 0)),
                  pl.BlockSpec((None, t_new, D), lambda b, g, pt: (2, rb0 + b, 0)),
                  pl.BlockSpec((t_new, LANES), lambda b, g, pt: (rb0 + b, 0)),
                  pl.BlockSpec((None, None, D, page), pidx),
                  pl.BlockSpec((None, None, D, page), pidx),
                  pl.BlockSpec((None, None, C_HEADS, page), pidx),
                  pl.BlockSpec(memory_space=pl.ANY)],
        out_specs=pl.BlockSpec((t_new, D), lambda b, g, pt: (rb0 + b, 0)),
        scratch_shapes=[pltpu.VMEM((nrow, D), BF16), pltpu.VMEM((nrow, 1), F32), pltpu.VMEM((nrow, 1), F32),
                        pltpu.VMEM((nrow, 1), F32), pltpu.VMEM((nrow, D), F32), pltpu.VMEM((C_HEADS, 1), F32)])
    return pl.pallas_call(
        functools.partial(_foxs_kernel, t_new=t_new),
        grid_spec=grid_spec,
        out_shape=jax.ShapeDtypeStruct((n, D), F32),
        input_output_aliases={8: 0},
        compiler_params=_cp("arbitrary", "arbitrary"),
        name="fox_sample",
    )(page_table.reshape(-1), qkv3, qkv3, qkv3, logf, kt, vt, clft, base)


def kernel(x_prompt, x_sample, c_prompt, c_sample, state_rwkv, state_rwkv_shift, state_retention, cache_fox_k, cache_fox_v, cache_fox_logf, page_table, w_ada, b_ada, norm_mix_g, norm_ffn_g, final_g, w_in_even, w_out_even, rw_mu, rw_w0, rw_w2, rw_a0, rw_a2, rw_g2, rw_kk, rw_ka, rw_rk, rw_ln_g, rw_ln_b, rw_v0, rw_v1, rw_v2, ret_gn_g, w_in_odd, b_forget, w_out_odd, w_router, b_router, w_gu, b_gu, w_down, b_down):
    bp, seq, _ = x_prompt.shape
    bs, tn_, _ = x_sample.shape
    n_p, n_s = bp * seq, bs * tn_
    depth = w_ada.shape[0]
    n_phys, page = cache_fox_k.shape[1], cache_fox_k.shape[2]
    npg = page_table.shape[1]
    past_len = npg * page
    n_odd = cache_fox_k.shape[0]
    kt_all = jnp.transpose(cache_fox_k, (0, 1, 3, 4, 2)).reshape(n_odd, n_phys, D, page)
    vt_all = jnp.transpose(cache_fox_v, (0, 1, 3, 4, 2)).reshape(n_odd, n_phys, D, page)
    clft_all = jnp.transpose(cache_fox_logf, (0, 1, 3, 2))

    x = jnp.concatenate([x_prompt.reshape(n_p, D), x_sample.reshape(n_s, D)], axis=0)
    ada = _ada(jnp.concatenate([c_prompt, c_sample], axis=0), w_ada, b_ada)

    zero_shift = jnp.zeros((bp, RWKV_PROJ), F32)
    zero_rw = jnp.zeros((bp, A_HEADS, A_HD, A_HD), F32)
    zero_ret = jnp.zeros((bp, B_HEADS, B_HD, B_HD), F32)
    new = {k: [] for k in ('rw_S_p', 'rw_S_s', 'sh_p', 'sh_s', 'ret_p', 'ret_s',
                           'k_p', 'k_s', 'v_p', 'v_s', 'lf_p', 'lf_s')}
    vf = None
    for i in range(depth):
        modp = ada[i, :, :bp].reshape(6, bp, 1, D)
        mods = jnp.repeat(ada[i, :, bp:], tn_, axis=1)
        if i % 2 == 0:
            e = i // 2
            w_in = w_in_even[e]
            w_pad = jnp.concatenate([w_in[:, :RWKV_PROJ], jnp.zeros((D, 2048 - RWKV_PROJ), F32),
                                     w_in[:, RWKV_PROJ:]], axis=1).astype(BF16)
            proj3 = _inproj(x, norm_mix_g[i], modp, mods, w_pad, 2048, seq)[0]
            lp = {'mu': rw_mu[e], 'w0': rw_w0[e], 'w2': rw_w2[e], 'a0': rw_a0[e], 'a2': rw_a2[e],
                  'g2': rw_g2[e], 'k_k': rw_kk[e], 'k_a': rw_ka[e], 'r_k': rw_rk[e],
                  'ln_g': rw_ln_g[e], 'ln_b': rw_ln_b[e]}
            vres = None if e == 0 else (rw_v0[e - 1], rw_v1[e - 1], rw_v2[e - 1])
            oa, s_p, sh_p, vf1 = _rwkv(proj3, 0, bp, seq, 1, RW_TC, zero_shift, zero_rw, vf, lp, vres, None)
            oa, s_s, sh_s, vf1 = _rwkv(proj3, n_p, bs, tn_, 8, tn_, state_rwkv_shift[e], state_rwkv[e], vf, lp,
                                       vres, (oa, vf1))
            if e == 0:
                vf = vf1
            ob, r_p = _retention(proj3, 0, bp, seq, 1, RET_L, 0, zero_ret, ret_gn_g[e], None)
            ob, r_s = _retention(proj3, n_p, bs, tn_, 8, tn_, past_len, state_retention[e], ret_gn_g[e], ob)
            new['rw_S_p'].append(s_p)
            new['rw_S_s'].append(s_s)
            new['sh_p'].append(sh_p)
            new['sh_s'].append(sh_s)
            new['ret_p'].append(r_p)
            new['ret_s'].append(r_s)
            w_out = w_out_even[e].astype(BF16)
            mixes, ws = [oa, ob], [w_out[:A_W], w_out[A_W:]]
        else:
            j = i // 2
            w_in = w_in_odd[j]
            we = jnp.pad(w_in[:, 3 * D:], ((0, 0), (0, LANES - C_HEADS))).astype(BF16)
            be = jnp.pad(b_forget[j], (0, LANES - C_HEADS)).reshape(1, LANES)
            qkv3, logf = _inproj(x, norm_mix_g[i], modp, mods, w_in[:, :3 * D].astype(BF16), D, seq, (we, be))
            c = _cumsum_prompt(logf, bp, seq)
            ct = jnp.transpose(c[:n_p, :C_HEADS].reshape(bp, seq, C_HEADS), (0, 2, 1))
            o = _fox_prompt(qkv3, c, ct, bp, seq)
            k_s = qkv3[1, n_p:].reshape(bs, tn_, C_HEADS, C_HD)
            v_s = qkv3[2, n_p:].reshape(bs, tn_, C_HEADS, C_HD)
            o = _fox_sample(qkv3, logf, n_p, bs, tn_, page_table, j, kt_all, vt_all, clft_all, o)
            new['k_p'].append(qkv3[1, :n_p].reshape(bp, seq, C_HEADS, C_HD))
            new['k_s'].append(k_s)
            new['v_p'].append(qkv3[2, :n_p].reshape(bp, seq, C_HEADS, C_HD))
            new['v_s'].append(v_s)
            new['lf_p'].append(logf[:n_p, :C_HEADS].reshape(bp, seq, C_HEADS))
            new['lf_s'].append(logf[n_p:, :C_HEADS].reshape(bs, tn_, C_HEADS))
            mixes, ws = [o], [w_out_odd[j].astype(BF16)]
        wr = jnp.pad(w_router[i], ((0, 0), (0, LANES - N_EXPERTS)))
        br = jnp.concatenate([b_router[i], jnp.full((LANES - N_EXPERTS,), NEG, F32)]).reshape(1, LANES)
        x, h, gates, idx = _outproj(x, mixes, ws, modp, mods, norm_ffn_g[i], wr, br, seq)
        slot_tok, slot_gate, block_exp, nused, slot_of = _moe_meta(idx[:, :TOP_K], gates[:, :TOP_K], MOE_TM)
        y_slots = _moe_experts(h, slot_tok, block_exp, nused, slot_gate, i, w_gu, b_gu, w_down, b_down)
        x = _combine(x, y_slots, slot_of, modp, mods, final_g, seq, i == depth - 1)

    st = lambda name: jnp.stack(new[name])
    return (x[:n_p].reshape(bp, seq, D), x[n_p:].reshape(bs, tn_, D),
            st('rw_S_p'), st('rw_S_s'), st('sh_p'), st('sh_s'), st('ret_p'), st('ret_s'),
            st('k_p'), st('k_s'), st('v_p'), st('v_s'), st('lf_p'), st('lf_s'))
```

```python
import functools
import math

import numpy as np
import jax
import jax.numpy as jnp
from jax import lax
from jax.experimental import pallas as pl
from jax.experimental.pallas import tpu as pltpu

F32 = jnp.float32
BF16 = jnp.bfloat16
HI = lax.Precision.HIGHEST

D = 1024
NORM_EPS = 1e-6
A_W = 512
A_HEADS = 8
A_HD = 64
RWKV_PROJ = 1792
LNX_EPS = 64e-5
B_HEADS = 4
B_HD = 128
GN_EPS = 1e-5
C_HEADS = 16
C_HD = 64
N_EXPERTS = 32
TOP_K = 4
SWIGLU_LIMIT = 7.0
SWIGLU_ALPHA = 1.702

V7X_VMEM_BYTES = 64 * 1024 * 1024
VMEM_LIMIT = V7X_VMEM_BYTES - 8 * 1024 * 1024
LANES = 128
NEG = -1e30

TM = 512
MOE_TM = 512
CMB_TM = 256
RW_NSEQ = 8
RW_TC = 64
RET_L = 256
FOX_TQ = 512
FOX_PAGES = 4


def _cp(*sem):
    return pltpu.CompilerParams(dimension_semantics=sem, vmem_limit_bytes=VMEM_LIMIT)


def _sigmoid(x):
    return 1.0 / (1.0 + jnp.exp(-x))


def _softplus(x):
    return jnp.maximum(x, 0.0) + jnp.log(1.0 + jnp.exp(-jnp.abs(x)))


def _bdot(a, b):
    return jnp.dot(a.astype(BF16), b.astype(BF16), preferred_element_type=F32)


def _hdot(a, b):
    return jnp.dot(a, b, precision=HI, preferred_element_type=F32)


def _nt_dot(a, b):
    return lax.dot_general(a, b, (((1,), (1,)), ((), ())), preferred_element_type=F32)


def _rms(x, g):
    return x * lax.rsqrt(jnp.mean(x * x, axis=-1, keepdims=True) + NORM_EPS) * g


def _ada_kernel(c_ref, w_ref, b_ref, o_ref):
    c = c_ref[...]
    o_ref[...] = _bdot(c * _sigmoid(c), w_ref[...]) + b_ref[...]


def _ada(c_all, w_ada, b_ada):
    nl = w_ada.shape[0]
    m = c_all.shape[0]
    return pl.pallas_call(
        _ada_kernel,
        grid=(nl, 6),
        in_specs=[pl.BlockSpec((m, D), lambda l, j: (0, 0)),
                  pl.BlockSpec((None, D, D), lambda l, j: (l, 0, j)),
                  pl.BlockSpec((None, None, 1, D), lambda l, j: (l, j, 0, 0))],
        out_specs=pl.BlockSpec((None, None, m, D), lambda l, j: (l, j, 0, 0)),
        out_shape=jax.ShapeDtypeStruct((nl, 6, m, D), F32),
        compiler_params=_cp("arbitrary", "arbitrary"),
        name="ada",
    )(c_all, w_ada, b_ada.reshape(nl, 6, 1, D))


def _modp_spec(which, tm, seq, nb):
    tps = seq // tm
    return pl.BlockSpec((None, None, 1, D), lambda i, *_: (which, jnp.minimum(i // tps, nb - 1), 0, 0))


def _mods_spec(which, tm, npt):
    return pl.BlockSpec((None, tm, D), lambda i, *_: (which, jnp.maximum(i - npt, 0), 0))


def _inproj_kernel(x_ref, g_ref, sh, sc, w_ref, *rest, has_extra):
    if has_extra:
        we_ref, be_ref, o_ref, oe_ref, h_scr = rest
    else:
        o_ref, h_scr = rest

    @pl.when(pl.program_id(1) == 0)
    def _():
        h = (_rms(x_ref[...], g_ref[...]) * (1.0 + sc[...]) + sh[...]).astype(BF16)
        h_scr[...] = h
        if has_extra:
            f = jnp.dot(h, we_ref[...], preferred_element_type=F32) + be_ref[...]
            oe_ref[...] = jnp.minimum(f, 0.0) - jnp.log(1.0 + jnp.exp(-jnp.abs(f)))

    o_ref[...] = jnp.dot(h_scr[...], w_ref[...], preferred_element_type=F32)


def _inproj(x, g, mod, per_token, tile0, ntiles, seq, w, tn, extra=None):
    nj = w.shape[1] // tn
    rows = ntiles * TM
    has_extra = extra is not None
    if per_token:
        mspec = lambda which: pl.BlockSpec((None, TM, D), lambda i, j: (which, i, 0))
    else:
        tps = seq // TM
        mspec = lambda which: pl.BlockSpec((None, None, 1, D), lambda i, j: (which, i // tps, 0, 0))
    in_specs = [pl.BlockSpec((TM, D), lambda i, j: (tile0 + i, 0)),
                pl.BlockSpec((1, D), lambda i, j: (0, 0)),
                mspec(0), mspec(1),
                pl.BlockSpec((D, tn), lambda i, j: (0, j))]
    args = [x, g.reshape(1, D), mod, mod, w]
    out_specs = [pl.BlockSpec((None, TM, tn), lambda i, j: (j, i, 0))]
    out_shape = [jax.ShapeDtypeStruct((nj, rows, tn), F32)]
    if has_extra:
        we, be = extra
        in_specs += [pl.BlockSpec((D, LANES), lambda i, j: (0, 0)),
                     pl.BlockSpec((1, LANES), lambda i, j: (0, 0))]
        args += [we, be]
        out_specs.append(pl.BlockSpec((TM, LANES), lambda i, j: (i, 0)))
        out_shape.append(jax.ShapeDtypeStruct((rows, LANES), F32))
    return pl.pallas_call(
        functools.partial(_inproj_kernel, has_extra=has_extra),
        grid=(ntiles, nj),
        in_specs=in_specs, out_specs=out_specs, out_shape=out_shape,
        scratch_shapes=[pltpu.VMEM((TM, D), BF16)],
        compiler_params=_cp("arbitrary", "arbitrary"),
        name="inproj",
    )(*args)


def _outproj_kernel(*refs, nparts, npt):
    x_ref = refs[0]
    mixp = refs[1:1 + nparts]
    mixs = refs[1 + nparts:1 + 2 * nparts]
    ws = refs[1 + 2 * nparts:1 + 3 * nparts]
    (gp, gs, nfg, shp, scp, shs, scs, wr, br, xo, ho, go, io, po, co, cnt_s) = refs[1 + 3 * nparts:]
    i = pl.program_id(0)
    isp = i < npt

    @pl.when(i == 0)
    def _():
        cnt_s[...] = jnp.zeros(cnt_s.shape, F32)

    acc = None
    for mp, ms, w in zip(mixp, mixs, ws):
        part = _bdot(jnp.where(isp, mp[...], ms[...]), w[...])
        acc = part if acc is None else acc + part
    xn = x_ref[...] + jnp.where(isp, gp[...], gs[...]) * acc
    xo[...] = xn
    y = _rms(xn, nfg[...])
    h = y * (1.0 + jnp.where(isp, scp[...], scs[...])) + jnp.where(isp, shp[...], shs[...])
    ho[...] = h
    logits = _hdot(h, wr[...]) + br[...]
    tm = logits.shape[0]
    lane = lax.broadcasted_iota(jnp.int32, logits.shape, 1)
    l = logits
    vals, idxs = [], []
    for _ in range(TOP_K):
        m = jnp.max(l, axis=-1, keepdims=True)
        ix = jnp.min(jnp.where(l == m, lane, LANES), axis=-1, keepdims=True)
        vals.append(m)
        idxs.append(ix)
        l = jnp.where(lane == ix, -jnp.inf, l)
    es = [jnp.exp(v - vals[0]) for v in vals]
    den = es[0] + es[1] + es[2] + es[3]
    dense = jnp.zeros(logits.shape, F32)
    for k in range(TOP_K):
        dense = jnp.where(lane == idxs[k], 1.0, dense)
    before = (lax.broadcasted_iota(jnp.int32, (tm, tm), 0) > lax.broadcasted_iota(jnp.int32, (tm, tm), 1))
    rank = jnp.dot(jnp.where(before, 1.0, 0.0).astype(BF16), dense.astype(BF16),
                   preferred_element_type=F32) + cnt_s[...]
    cnt = cnt_s[...] + jnp.sum(dense, axis=0, keepdims=True)
    cnt_s[...] = cnt
    co[...] = cnt
    gates = jnp.zeros(logits.shape, F32)
    idxo = jnp.zeros(logits.shape, jnp.int32)
    poso = jnp.zeros(logits.shape, jnp.int32)
    for k in range(TOP_K):
        pk = jnp.sum(jnp.where(lane == idxs[k], rank, 0.0), axis=-1, keepdims=True)
        gates = jnp.where(lane == k, es[k] / den, gates)
        idxo = jnp.where(lane == k, idxs[k], idxo)
        poso = jnp.where(lane == k, pk.astype(jnp.int32), poso)
    go[...] = gates
    io[...] = idxo
    po[...] = poso


def _outproj(x, mixes_p, mixes_s, ws, modp, mods, nfg, wr, br, seq):
    n = x.shape[0]
    n_s = mods.shape[1]
    npt = (n - n_s) // TM
    nb = modp.shape[1]
    nparts = len(ws)
    row = lambda i: (i, 0)
    const = lambda i: (0, 0)
    in_specs = [pl.BlockSpec((TM, D), row)]
    in_specs += [pl.BlockSpec((TM, m.shape[1]), lambda i: (jnp.minimum(i, npt - 1), 0)) for m in mixes_p]
    in_specs += [pl.BlockSpec((TM, m.shape[1]), lambda i: (jnp.maximum(i - npt, 0), 0)) for m in mixes_s]
    in_specs += [pl.BlockSpec(w.shape, const) for w in ws]
    in_specs += [_modp_spec(2, TM, seq, nb), _mods_spec(2, TM, npt),
                 pl.BlockSpec((1, D), const),
                 _modp_spec(3, TM, seq, nb), _modp_spec(4, TM, seq, nb),
                 _mods_spec(3, TM, npt), _mods_spec(4, TM, npt),
                 pl.BlockSpec((D, LANES), const), pl.BlockSpec((1, LANES), const)]
    args = [x, *mixes_p, *mixes_s, *ws, modp, mods, nfg.reshape(1, D), modp, modp, mods, mods, wr, br]
    return pl.pallas_call(
        functools.partial(_outproj_kernel, nparts=nparts, npt=npt),
        grid=(n // TM,),
        in_specs=in_specs,
        out_specs=[pl.BlockSpec((TM, D), row), pl.BlockSpec((TM, D), row),
                   pl.BlockSpec((TM, LANES), row), pl.BlockSpec((TM, LANES), row),
                   pl.BlockSpec((TM, LANES), row), pl.BlockSpec((1, LANES), const)],
        out_shape=[jax.ShapeDtypeStruct((n, D), F32), jax.ShapeDtypeStruct((n, D), F32),
                   jax.ShapeDtypeStruct((n, LANES), F32), jax.ShapeDtypeStruct((n, LANES), jnp.int32),
                   jax.ShapeDtypeStruct((n, LANES), jnp.int32), jax.ShapeDtypeStruct((1, LANES), F32)],
        scratch_shapes=[pltpu.VMEM((1, LANES), F32)],
        compiler_params=_cp("arbitrary"),
        name="outproj",
    )(*args)


def _moe_kernel(tok_ref, bexp_ref, nused_ref, h_hbm, wgu_ref, bgu_ref, wd_ref, bd_ref,
                o_ref, x0, x1, wgu_s, wd_s, sem, *, tm):
    blk = pl.program_id(0)
    nblk = pl.num_programs(0)
    nu = nused_ref[0]

    def row_copy(b, r, buf, s):
        return pltpu.make_async_copy(h_hbm.at[pl.ds(tok_ref[b * tm + r], 1)], buf.at[pl.ds(r, 1)], sem.at[s])

    def issue_loop(b, buf, s):
        def body(r, c):
            row_copy(b, r, buf, s).start()
            return c
        lax.fori_loop(0, tm, body, 0)

    @pl.when(blk == 0)
    def _():
        issue_loop(0, x0, 0)

    def step(cur, cs, nxt, ns):
        pltpu.make_async_copy(h_hbm.at[pl.ds(0, tm)], cur, sem.at[cs]).wait()

        @pl.when(blk < nu)
        def _():
            e = bexp_ref[blk]
            e_prev = bexp_ref[jnp.maximum(blk - 1, 0)]

            @pl.when((blk == 0) | (e != e_prev))
            def _():
                wgu_s[...] = wgu_ref[...].astype(BF16)
                wd_s[...] = wd_ref[...].astype(BF16)

            for r in range(tm):
                row_copy(blk + 1, r, nxt, ns).start()
            xb = cur[...].astype(BF16)
            gu = jnp.dot(xb, wgu_s[...], preferred_element_type=F32) + bgu_ref[...]
            gate = jnp.minimum(gu[:, :D], SWIGLU_LIMIT)
            up = jnp.clip(gu[:, D:], -SWIGLU_LIMIT, SWIGLU_LIMIT)
            act = (up + 1.0) * gate * _sigmoid(SWIGLU_ALPHA * gate)
            o_ref[...] = jnp.dot(act.astype(BF16), wd_s[...], preferred_element_type=F32) + bd_ref[...]

        @pl.when(blk >= nu)
        def _():
            @pl.when(blk + 1 < nblk)
            def _():
                issue_loop(blk + 1, nxt, ns)
            o_ref[...] = jnp.zeros(o_ref.shape, F32)

    @pl.when(blk % 2 == 0)
    def _():
        step(x0, 0, x1, 1)

    @pl.when(blk % 2 == 1)
    def _():
        step(x1, 1, x0, 0)


def _moe_experts(h, slot_tok, block_exp, nused, layer, w_gu, b_gu, w_down, b_down):
    tm = MOE_TM
    nb = block_exp.shape[0]
    nl = w_gu.shape[0]
    grid_spec = pltpu.PrefetchScalarGridSpec(
        num_scalar_prefetch=3,
        grid=(nb,),
        in_specs=[pl.BlockSpec(memory_space=pl.ANY),
                  pl.BlockSpec((None, None, D, 2 * D), lambda b, t, e, u: (layer, e[b], 0, 0)),
                  pl.BlockSpec((None, None, 1, 2 * D), lambda b, t, e, u: (layer, e[b], 0, 0)),
                  pl.BlockSpec((None, None, D, D), lambda b, t, e, u: (layer, e[b], 0, 0)),
                  pl.BlockSpec((None, None, 1, D), lambda b, t, e, u: (layer, e[b], 0, 0))],
        out_specs=pl.BlockSpec((tm, D), lambda b, t, e, u: (b, 0)),
        scratch_shapes=[pltpu.VMEM((tm, D), F32), pltpu.VMEM((tm, D), F32),
                        pltpu.VMEM((D, 2 * D), BF16),
                        pltpu.VMEM((D, D), BF16),
                        pltpu.SemaphoreType.DMA((2,))])
    return pl.pallas_call(
        functools.partial(_moe_kernel, tm=tm),
        grid_spec=grid_spec,
        out_shape=jax.ShapeDtypeStruct((nb * tm, D), F32),
        compiler_params=_cp("arbitrary"),
        name="moe_experts",
    )(slot_tok, block_exp, nused, h, w_gu,
      b_gu.reshape(nl, N_EXPERTS, 1, 2 * D), w_down, b_down.reshape(nl, N_EXPERTS, 1, D))


def _combine_kernel(sof_ref, y_hbm, x_ref, gt_ref, gp, gs, fg_ref, o_ref, ybuf, sem, *, tm, npt, final):
    i = pl.program_id(0)
    nt = pl.num_programs(0)

    def issue(t, slot):
        base = t * (tm * TOP_K)

        def body(r, c):
            for k in range(TOP_K):
                s = sof_ref[base + r * TOP_K + k]
                pltpu.make_async_copy(y_hbm.at[pl.ds(s, 1)], ybuf.at[slot, k, pl.ds(r, 1)], sem.at[slot]).start()
            return c

        lax.fori_loop(0, tm, body, 0, unroll=4)

    @pl.when(i == 0)
    def _():
        issue(0, 0)

    @pl.when(i + 1 < nt)
    def _():
        issue(i + 1, (i + 1) % 2)

    slot = i % 2
    for k in range(TOP_K):
        pltpu.make_async_copy(y_hbm.at[pl.ds(0, tm)], ybuf.at[slot, k], sem.at[slot]).wait()
    gt = gt_ref[...]
    y = ((ybuf[slot, 0] * gt[:, 0:1] + ybuf[slot, 1] * gt[:, 1:2])
         + (ybuf[slot, 2] * gt[:, 2:3] + ybuf[slot, 3] * gt[:, 3:4]))
    xn = x_ref[...] + jnp.where(i < npt, gp[...], gs[...]) * y
    if final:
        xn = _rms(xn, fg_ref[...])
    o_ref[...] = xn


def _combine(x, y_slots, slot_of, gates, modp, mods, final_g, seq, final):
    tm = CMB_TM
    n = x.shape[0]
    n_s = mods.shape[1]
    npt = (n - n_s) // tm
    nb = modp.shape[1]
    grid_spec = pltpu.PrefetchScalarGridSpec(
        num_scalar_prefetch=1,
        grid=(n // tm,),
        in_specs=[pl.BlockSpec(memory_space=pl.ANY),
                  pl.BlockSpec((tm, D), lambda i, s: (i, 0)),
                  pl.BlockSpec((tm, LANES), lambda i, s: (i, 0)),
                  _modp_spec(5, tm, seq, nb), _mods_spec(5, tm, npt),
                  pl.BlockSpec((1, D), lambda i, s: (0, 0))],
        out_specs=pl.BlockSpec((tm, D), lambda i, s: (i, 0)),
        scratch_shapes=[pltpu.VMEM((2, TOP_K, tm, D), F32), pltpu.SemaphoreType.DMA((2,))])
    return pl.pallas_call(
        functools.partial(_combine_kernel, tm=tm, npt=npt, final=final),
        grid_spec=grid_spec,
        out_shape=jax.ShapeDtypeStruct((n, D), F32),
        compiler_params=_cp("arbitrary"),
        name="moe_combine",
    )(slot_of, y_slots, x, gates, modp, mods, final_g.reshape(1, D))


def _moe_meta(idx4, pos4, counts, tm):
    n = idx4.shape[0]
    nk = n * TOP_K
    nb = -(-(nk + N_EXPERTS * (tm - 1)) // tm) + 1
    padded = (counts + tm - 1) // tm * tm
    pad_end = jnp.cumsum(padded)
    pad_start = pad_end - padded
    flat = (pad_start[idx4] + pos4).astype(jnp.int32).reshape(nk)
    tok = jnp.repeat(jnp.arange(n, dtype=jnp.int32), TOP_K)
    slot_tok = jnp.zeros((nb * tm,), jnp.int32).at[flat].set(tok)
    block_exp = jnp.minimum(jnp.searchsorted(pad_end, jnp.arange(nb, dtype=jnp.int32) * tm, side='right'),
                            N_EXPERTS - 1).astype(jnp.int32)
    nused = (pad_end[-1] // tm).astype(jnp.int32).reshape(1)
    return slot_tok, block_exp, nused, flat


def _seg_consts():
    r = np.arange(256)
    mseg = (r[:, None] // A_HD == r[None, :] // A_HD).astype(np.float32)
    c = np.arange(A_W)
    eyet = (np.arange(A_HD)[:, None] == (c[None, :] % A_HD)).astype(np.float32)
    return jnp.asarray(mseg, BF16), jnp.asarray(eyet, F32)


def _split(x):
    hi = x.astype(BF16)
    lo = (x - hi.astype(F32)).astype(BF16)
    return hi, lo


def _segsum(x, mseg):
    hi, lo = _split(x)
    halves = []
    for hf in range(2):
        sl = slice(hf * 256, (hf + 1) * 256)
        halves.append(jnp.dot(hi[:, sl], mseg, preferred_element_type=F32)
                      + jnp.dot(lo[:, sl], mseg, preferred_element_type=F32))
    return jnp.concatenate(halves, axis=1)


def _rwkv_kernel(*refs, nseq, tc, has_vres):
    it = iter(refs)
    p_ref, st_ref, s0_ref = next(it), next(it), next(it)
    vf_ref = next(it) if has_vres else None
    mu, w0, w2, a0, a2, g2, kkp, kap, rk, lng, lnb = (next(it) for _ in range(11))
    if has_vres:
        v0, v1, v2 = next(it), next(it), next(it)
    mseg_ref, eyet_ref = next(it), next(it)
    o_ref, sout_ref, shout_ref = next(it), next(it), next(it)
    vfo_ref = None if has_vres else next(it)
    w_s, k_s, v_s, nkk_s, b_s, r_s, out_s, st_s, prev_s = (next(it) for _ in range(9))

    c = pl.program_id(1)
    nch = pl.num_programs(1)
    rows = nseq * tc
    mseg = mseg_ref[...]
    eyet = eyet_ref[...]

    @pl.when(c == 0)
    def _():
        prev_s[...] = st_ref[...]
        for s in range(nseq):
            for h in range(A_HEADS):
                st_s[s, :, h * A_HD:(h + 1) * A_HD] = s0_ref[s, h]

    p = p_ref[:, :, :RWKV_PROJ].reshape(rows, RWKV_PROJ)
    rolled = pltpu.roll(p, 1, 0)
    ridx = lax.broadcasted_iota(jnp.int32, (rows, 1), 0)
    sidx = lax.broadcasted_iota(jnp.int32, (rows, nseq), 1)
    onehot = (lax.broadcasted_iota(jnp.int32, (rows, nseq), 0) == sidx * tc).astype(F32)
    prev = jnp.where(ridx % tc == 0, _hdot(onehot, prev_s[...]), rolled)
    z = p + (prev - p) * mu[...]
    r = z[:, 0:512]
    k = z[:, 512:1024]
    v = z[:, 1024:1536]
    wi = z[:, 1536:1600]
    ai = z[:, 1600:1664]
    gi = z[:, 1664:1792]
    w_log = -_softplus(-(w0[...] + _hdot(jnp.tanh(wi), w2[...]))) - 0.5
    w = jnp.exp(-jnp.exp(w_log))
    a = _sigmoid(a0[...] + _hdot(ai, a2[...]))
    g = _bdot(_sigmoid(gi), g2[...])
    if has_vres:
        vf = vf_ref[...].reshape(rows, A_W)
        v = v + (vf - v) * _sigmoid(v0[...] + _bdot(_bdot(v, v1[...]), v2[...]))
    else:
        vfo_ref[...] = v.reshape(nseq, tc, A_W)
    kk = k * kkp[...]
    kk = kk * lax.rsqrt(jnp.maximum(_segsum(kk * kk, mseg), 1e-24))
    k = k * (1.0 + (a - 1.0) * kap[...])
    w_s[...] = w
    k_s[...] = k
    v_s[...] = v
    nkk_s[...] = -kk
    b_s[...] = kk * a
    r_s[...] = r
    bonus = _segsum(r * k * rk[...], mseg) * v

    def halves(x):
        return [x[:, 0:256], x[:, 256:512]]

    def token(t, carry):
        lhs = []
        for s in range(nseq):
            row = s * tc + t
            st = st_s[s]
            x1h, x1l = _split(st * nkk_s[pl.ds(row, 1), :])
            dv = (eyet * v_s[pl.ds(row, 1), :]).astype(BF16)
            lhs += halves(x1h) + halves(x1l) + halves(dv)
        res = jnp.dot(jnp.concatenate(lhs, axis=0), mseg, preferred_element_type=F32)
        lhs2 = []
        for s in range(nseq):
            row = s * tc + t
            o = s * 6 * A_HD
            pc = [res[o + q * A_HD:o + (q + 1) * A_HD] for q in range(6)]
            sa = jnp.concatenate([pc[0] + pc[2], pc[1] + pc[3]], axis=1)
            v2 = jnp.concatenate([pc[4], pc[5]], axis=1)
            st = (st_s[s] * w_s[pl.ds(row, 1), :] + sa * b_s[pl.ds(row, 1), :]
                  + v2 * k_s[pl.ds(row, 1), :])
            st_s[s] = st
            lhs2 += halves((st * r_s[pl.ds(row, 1), :]).astype(BF16))
        res2 = jnp.dot(jnp.concatenate(lhs2, axis=0), mseg, preferred_element_type=F32)
        for s in range(nseq):
            row = s * tc + t
            o = s * 2 * A_HD
            o2 = jnp.concatenate([res2[o:o + A_HD], res2[o + A_HD:o + 2 * A_HD]], axis=1)
            out_s[pl.ds(row, 1), :] = jnp.sum(o2 * eyet, axis=0, keepdims=True)
        return carry

    lax.fori_loop(0, tc, token, 0)

    out = out_s[...]
    xc = out - _segsum(out, mseg) * (1.0 / A_HD)
    var = _segsum(xc * xc, mseg) * (1.0 / A_HD)
    y = xc * lax.rsqrt(var + LNX_EPS) * lng[...] + lnb[...]
    o_ref[...] = ((y + bonus) * g).reshape(nseq, tc, A_W)

    for s in range(nseq):
        prev_s[s:s + 1, :] = p[s * tc + tc - 1:s * tc + tc, :]

    @pl.when(c == nch - 1)
    def _():
        shout_ref[...] = prev_s[...]
        for s in range(nseq):
            for h in range(A_HEADS):
                sout_ref[s, h] = st_s[s, :, h * A_HD:(h + 1) * A_HD]


def _rwkv(proj, nb, seq, nseq, tc, st_shift, s0, vfirst, lp, vres):
    nch = seq // tc
    rows = nseq * tc
    has_vres = vres is not None
    blk3 = lambda b, c: (b, c, 0)
    vec = lambda a: a.reshape(1, -1)
    c2 = lambda b, c: (0, 0)
    mseg, eyet = _seg_consts()
    in_specs = [pl.BlockSpec((None, nseq, tc, 2048), lambda b, c: (0, b, c, 0)),
                pl.BlockSpec((None, nseq, RWKV_PROJ), lambda b, c: (b, 0, 0)),
                pl.BlockSpec((nseq, A_HEADS, A_HD, A_HD), lambda b, c: (b, 0, 0, 0))]
    args = [proj.reshape(proj.shape[0], nb, seq, 2048), st_shift.reshape(nb // nseq, nseq, RWKV_PROJ), s0]
    if has_vres:
        in_specs.append(pl.BlockSpec((nseq, tc, A_W), blk3))
        args.append(vfirst)
    small = [vec(lp['mu']), vec(lp['w0']), lp['w2'], vec(lp['a0']), lp['a2'], lp['g2'].astype(BF16),
             vec(lp['k_k']), vec(lp['k_a']), vec(lp['r_k']), vec(lp['ln_g']), vec(lp['ln_b'])]
    if has_vres:
        small += [vec(vres[0]), vres[1].astype(BF16), vres[2].astype(BF16)]
    small += [mseg, eyet]
    in_specs += [pl.BlockSpec(a.shape, c2) for a in small]
    args += small
    out_specs = [pl.BlockSpec((nseq, tc, A_W), blk3),
                 pl.BlockSpec((nseq, A_HEADS, A_HD, A_HD), lambda b, c: (b, 0, 0, 0)),
                 pl.BlockSpec((None, nseq, RWKV_PROJ), lambda b, c: (b, 0, 0))]
    out_shape = [jax.ShapeDtypeStruct((nb, seq, A_W), F32),
                 jax.ShapeDtypeStruct((nb, A_HEADS, A_HD, A_HD), F32),
                 jax.ShapeDtypeStruct((nb // nseq, nseq, RWKV_PROJ), F32)]
    if not has_vres:
        out_specs.append(pl.BlockSpec((nseq, tc, A_W), blk3))
        out_shape.append(jax.ShapeDtypeStruct((nb, seq, A_W), F32))
    scratch = [pltpu.VMEM((rows, A_W), F32) for _ in range(7)]
    scratch += [pltpu.VMEM((nseq, A_HD, A_W), F32), pltpu.VMEM((nseq, RWKV_PROJ), F32)]
    outs = pl.pallas_call(
        functools.partial(_rwkv_kernel, nseq=nseq, tc=tc, has_vres=has_vres),
        grid=(nb // nseq, nch),
        in_specs=in_specs, out_specs=out_specs, out_shape=out_shape,
        scratch_shapes=scratch,
        compiler_params=_cp("arbitrary", "arbitrary"),
        name="rwkv",
    )(*args)
    o, s_out, sh_out = outs[0].reshape(nb * seq, A_W), outs[1], outs[2].reshape(nb, RWKV_PROJ)
    vf_out = vfirst if has_vres else outs[3]
    return o, s_out, sh_out, vf_out


def _ret_kernel(pr_ref, s0_ref, cos_ref, sin_ref, gn_ref, o_ref, sout_ref, st_s, *, nseq, L):
    c = pl.program_id(1)
    nch = pl.num_programs(1)

    @pl.when(c == 0)
    def _():
        st_s[...] = s0_ref[...]

    cos = cos_ref[...]
    sin = sin_ref[...]
    ii = lax.broadcasted_iota(jnp.int32, (L, L), 0)
    jj = lax.broadcasted_iota(jnp.int32, (L, L), 1)
    dif = (ii - jj).astype(F32)
    ri = lax.broadcasted_iota(jnp.int32, (L, 1), 0).astype(F32)

    def rot(x):
        return x * cos + pltpu.roll(x, B_HD // 2, 1) * sin

    for h in range(B_HEADS):
        lg = math.log1p(-2.0 ** (-5.0 - h))
        intra = jnp.where(dif >= 0, jnp.exp(jnp.maximum(dif, 0.0) * lg), 0.0)
        q_dec = jnp.exp((ri + 1.0) * lg)
        k_dec = jnp.exp((L - 1.0 - ri) * lg)
        c_dec = math.exp(L * lg)
        gn = gn_ref[:, h * B_HD:(h + 1) * B_HD]
        for s in range(nseq):
            rs = slice(s * L, (s + 1) * L)
            q = rot(pr_ref[rs, h * B_HD:(h + 1) * B_HD])
            k = rot(pr_ref[rs, 512 + h * B_HD:512 + (h + 1) * B_HD]) * (B_HD ** -0.5)
            v = pr_ref[rs, 1024 + h * B_HD:1024 + (h + 1) * B_HD]
            g = pr_ref[rs, 1536 + h * B_HD:1536 + (h + 1) * B_HD]
            st = st_s[s, h]
            vb = v.astype(BF16)
            att = _nt_dot(q.astype(BF16), k.astype(BF16)) * intra
            o = (jnp.dot(att.astype(BF16), vb, preferred_element_type=F32)
                 + _bdot(q * q_dec, st))
            st_s[s, h] = st * c_dec + lax.dot_general(
                (k * k_dec).astype(BF16), vb, (((0,), (0,)), ((), ())), preferred_element_type=F32)
            oc = o - jnp.mean(o, axis=-1, keepdims=True)
            y = oc * lax.rsqrt(jnp.mean(oc * oc, axis=-1, keepdims=True) + GN_EPS) * gn
            o_ref[rs, h * B_HD:(h + 1) * B_HD] = g * _sigmoid(g) * y

    @pl.when(c == nch - 1)
    def _():
        sout_ref[...] = st_s[...]


def _rot_tables(pos0, t):
    half = B_HD // 2
    inv = 1.0 / (10000.0 ** jnp.linspace(0.0, 1.0, half, dtype=F32))
    ang = (pos0 + jnp.arange(t, dtype=F32))[:, None] * inv[None, :]
    cos, sin = jnp.cos(ang), jnp.sin(ang)
    return jnp.concatenate([cos, cos], axis=1), jnp.concatenate([-sin, sin], axis=1)


def _retention(proj, nb, seq, nseq, L, pos0, s0, gn_g):
    n = proj.shape[1]
    nch = seq // L
    rows = nseq * L
    cos, sin = _rot_tables(float(pos0), seq)
    rowmap = lambda b, c: (b * nch + c, 0)
    return pl.pallas_call(
        functools.partial(_ret_kernel, nseq=nseq, L=L),
        grid=(nb // nseq, nch),
        in_specs=[pl.BlockSpec((None, rows, 2048), lambda b, c: (1, b * nch + c, 0)),
                  pl.BlockSpec((nseq, B_HEADS, B_HD, B_HD), lambda b, c: (b, 0, 0, 0)),
                  pl.BlockSpec((L, B_HD), lambda b, c: (c, 0)),
                  pl.BlockSpec((L, B_HD), lambda b, c: (c, 0)),
                  pl.BlockSpec((1, 512), lambda b, c: (0, 0))],
        out_specs=[pl.BlockSpec((rows, 512), rowmap),
                   pl.BlockSpec((nseq, B_HEADS, B_HD, B_HD), lambda b, c: (b, 0, 0, 0))],
        out_shape=[jax.ShapeDtypeStruct((n, 512), F32),
                   jax.ShapeDtypeStruct((nb, B_HEADS, B_HD, B_HD), F32)],
        scratch_shapes=[pltpu.VMEM((nseq, B_HEADS, B_HD, B_HD), F32)],
        compiler_params=_cp("arbitrary", "arbitrary"),
        name="retention",
    )(proj, s0, cos, sin, gn_g.reshape(1, 512))


def _tri(n):
    return (lax.broadcasted_iota(jnp.int32, (n, n), 0) >= lax.broadcasted_iota(jnp.int32, (n, n), 1)).astype(F32)


def _cumsum_kernel(lf_ref, o_ref):
    t = lf_ref.shape[0]
    tri = _tri(LANES)
    carry = jnp.zeros((1, LANES), F32)
    for b in range(t // LANES):
        rs = slice(b * LANES, (b + 1) * LANES)
        cb = _hdot(tri, lf_ref[rs, :]) + carry
        o_ref[rs, :] = cb
        carry = cb[LANES - 1:LANES, :]


def _cumsum_prompt(logf, nb, seq):
    return pl.pallas_call(
        _cumsum_kernel,
        grid=(nb,),
        in_specs=[pl.BlockSpec((seq, LANES), lambda b: (b, 0))],
        out_specs=pl.BlockSpec((seq, LANES), lambda b: (b, 0)),
        out_shape=jax.ShapeDtypeStruct((nb * seq, LANES), F32),
        compiler_params=_cp("arbitrary"),
        name="fox_cumsum",
    )(logf)


def _foxp_kernel(q_ref, k_ref, v_ref, cq_ref, ck_ref, o_ref, *, tq):
    hp = pl.program_id(1)
    qi = pl.program_id(2)
    lane = lax.broadcasted_iota(jnp.int32, (tq, LANES), 1)
    qs, cqs = [], []
    for hh in range(2):
        hs = slice(hh * C_HD, (hh + 1) * C_HD)
        qs.append((q_ref[:, hs] * (C_HD ** -0.5)).astype(BF16))
        cqs.append(jnp.sum(jnp.where(lane == hp * 2 + hh, cq_ref[...], 0.0), axis=-1, keepdims=True))

    def block(off, diag, carry):
        new = []
        for hh in range(2):
            m, l, acc = carry[3 * hh:3 * hh + 3]
            hs = slice(hh * C_HD, (hh + 1) * C_HD)
            k = k_ref[pl.ds(off, tq), hs].astype(BF16)
            v = v_ref[pl.ds(off, tq), hs].astype(BF16)
            s = _nt_dot(qs[hh], k) + (cqs[hh] - ck_ref[pl.ds(hp * 2 + hh, 1), pl.ds(off, tq)])
            if diag:
                s = jnp.where(lax.broadcasted_iota(jnp.int32, (tq, tq), 0)
                              >= lax.broadcasted_iota(jnp.int32, (tq, tq), 1), s, NEG)
            m_new = jnp.maximum(m, jnp.max(s, axis=-1, keepdims=True))
            alpha = jnp.exp(m - m_new)
            p = jnp.exp(s - m_new)
            new += [m_new, alpha * l + jnp.sum(p, axis=-1, keepdims=True),
                    alpha * acc + jnp.dot(p.astype(BF16), v, preferred_element_type=F32)]
        return tuple(new)

    init = (jnp.full((tq, 1), NEG, F32), jnp.zeros((tq, 1), F32), jnp.zeros((tq, C_HD), F32)) * 2
    carry = lax.fori_loop(0, qi, lambda kj, c: block(pl.multiple_of(kj * tq, tq), False, c), init)
    carry = block(pl.multiple_of(qi * tq, tq), True, carry)
    o_ref[...] = jnp.concatenate([carry[2] / carry[1], carry[5] / carry[4]], axis=1)


def _fox_prompt(qkv, c, ct, nb, seq):
    n = qkv.shape[1]
    tq = FOX_TQ
    nq = seq // tq
    return pl.pallas_call(
        functools.partial(_foxp_kernel, tq=tq),
        grid=(nb, C_HEADS // 2, nq),
        in_specs=[pl.BlockSpec((None, tq, LANES), lambda b, hp, qi: (0, b * nq + qi, hp)),
                  pl.BlockSpec((None, seq, LANES), lambda b, hp, qi: (1, b, hp)),
                  pl.BlockSpec((None, seq, LANES), lambda b, hp, qi: (2, b, hp)),
                  pl.BlockSpec((tq, LANES), lambda b, hp, qi: (b * nq + qi, 0)),
                  pl.BlockSpec((None, C_HEADS, seq), lambda b, hp, qi: (b, 0, 0))],
        out_specs=pl.BlockSpec((tq, LANES), lambda b, hp, qi: (b * nq + qi, hp)),
        out_shape=jax.ShapeDtypeStruct((n, D), F32),
        compiler_params=_cp("arbitrary", "arbitrary", "arbitrary"),
        name="fox_prompt",
    )(qkv, qkv, qkv, c, ct)


def _foxs_kernel(pt_ref, q_ref, kn_ref, vn_ref, lfn_ref, *rest, t_new, npp):
    kts, vts, clfs = rest[0:npp], rest[npp:2 * npp], rest[2 * npp:3 * npp]
    o_ref, qbd, cnq_s, m_s, l_s, acc_s, carry = rest[3 * npp:]
    g = pl.program_id(1)
    nrow = C_HEADS * t_new
    page = clfs[0].shape[1]
    rowh = lax.broadcasted_iota(jnp.int32, (nrow, D), 0) // t_new
    bmask = (lax.broadcasted_iota(jnp.int32, (nrow, D), 1) // C_HD) == rowh

    def per_row(x):
        return jnp.concatenate([jnp.broadcast_to(x[h:h + 1, :], (t_new, x.shape[1])) for h in range(C_HEADS)],
                               axis=0)

    def update(s, pv):
        m = m_s[...]
        m_new = jnp.maximum(m, jnp.max(s, axis=-1, keepdims=True))
        alpha = jnp.exp(m - m_new)
        p = jnp.exp(s - m_new)
        l_s[...] = alpha * l_s[...] + jnp.sum(p, axis=-1, keepdims=True)
        acc_s[...] = alpha * acc_s[...] + pv(p)
        m_s[...] = m_new

    @pl.when(g == 0)
    def _():
        q = q_ref[...] * (C_HD ** -0.5)
        qbd[...] = jnp.where(bmask, jnp.concatenate([q] * C_HEADS, axis=0), 0.0).astype(BF16)
        m_s[...] = jnp.full(m_s.shape, NEG, F32)
        l_s[...] = jnp.zeros(l_s.shape, F32)
        acc_s[...] = jnp.zeros(acc_s.shape, F32)
        carry[...] = jnp.zeros(carry.shape, F32)
        tle = (lax.broadcasted_iota(jnp.int32, (t_new, t_new), 0)
               <= lax.broadcasted_iota(jnp.int32, (t_new, t_new), 1)).astype(F32)
        cnt = lax.dot_general(lfn_ref[:, :C_HEADS], tle, (((0,), (0,)), ((), ())),
                              precision=HI, preferred_element_type=F32)
        gk = per_row(cnt)
        colt = lax.broadcasted_iota(jnp.int32, (nrow, t_new), 1)
        qrow = lax.broadcasted_iota(jnp.int32, (nrow, t_new), 0) % t_new
        cnq = jnp.sum(jnp.where(colt == qrow, gk, 0.0), axis=-1, keepdims=True)
        cnq_s[...] = cnq
        s2 = _nt_dot(qbd[...], kn_ref[...].astype(BF16)) + cnq - gk
        update(jnp.where(colt <= qrow, s2, NEG),
               lambda p: jnp.dot(p, vn_ref[...], preferred_element_type=F32))

    later = (lax.broadcasted_iota(jnp.int32, (page, page), 0)
             > lax.broadcasted_iota(jnp.int32, (page, page), 1)).astype(F32)
    lfs = [r[...] for r in clfs]
    suf = _hdot(jnp.concatenate(lfs, axis=0), later)
    run = carry[...]
    ds = []
    for i in range(npp):
        si = suf[i * C_HEADS:(i + 1) * C_HEADS]
        ds.append(si + run)
        run = run + si[:, 0:1] + lfs[i][:, 0:1]
    carry[...] = run
    qb = qbd[...]
    s = jnp.concatenate([jnp.dot(qb, kt[...].astype(BF16), preferred_element_type=F32) for kt in kts], axis=1)
    s = s + (per_row(jnp.concatenate(ds, axis=1)) + cnq_s[...])

    def pv(p):
        pb = p.astype(BF16)
        out = _nt_dot(pb[:, 0:page], vts[0][...].astype(BF16))
        for i in range(1, npp):
            out = out + _nt_dot(pb[:, i * page:(i + 1) * page], vts[i][...].astype(BF16))
        return out

    update(s, pv)

    @pl.when(g == pl.num_programs(1) - 1)
    def _():
        o = jnp.where(bmask, acc_s[...] / l_s[...], 0.0)
        o_ref[...] = jnp.sum(o.reshape(C_HEADS, t_new, D), axis=0)


def _fox_sample(qkv, logf, nb, t_new, page_table, layer, kt, vt, clft):
    n = qkv.shape[1]
    npg = page_table.shape[1]
    page = kt.shape[3]
    nrow = C_HEADS * t_new
    npp = FOX_PAGES

    def pidx(i):
        return lambda b, g, pt: (layer, pt[b * npg + (npg - 1 - (g * npp + i))], 0, 0)

    in_specs = [pl.BlockSpec((None, t_new, D), lambda b, g, pt: (0, b, 0)),
                pl.BlockSpec((None, t_new, D), lambda b, g, pt: (1, b, 0)),
                pl.BlockSpec((None, t_new, D), lambda b, g, pt: (2, b, 0)),
                pl.BlockSpec((t_new, LANES), lambda b, g, pt: (b, 0))]
    in_specs += [pl.BlockSpec((None, None, D, page), pidx(i)) for i in range(npp)]
    in_specs += [pl.BlockSpec((None, None, D, page), pidx(i)) for i in range(npp)]
    in_specs += [pl.BlockSpec((None, None, C_HEADS, page), pidx(i)) for i in range(npp)]
    grid_spec = pltpu.PrefetchScalarGridSpec(
        num_scalar_prefetch=1,
        grid=(nb, npg // npp),
        in_specs=in_specs,
        out_specs=pl.BlockSpec((t_new, D), lambda b, g, pt: (b, 0)),
        scratch_shapes=[pltpu.VMEM((nrow, D), BF16), pltpu.VMEM((nrow, 1), F32), pltpu.VMEM((nrow, 1), F32),
                        pltpu.VMEM((nrow, 1), F32), pltpu.VMEM((nrow, D), F32), pltpu.VMEM((C_HEADS, 1), F32)])
    return pl.pallas_call(
        functools.partial(_foxs_kernel, t_new=t_new, npp=npp),
        grid_spec=grid_spec,
        out_shape=jax.ShapeDtypeStruct((n, D), F32),
        compiler_params=_cp("arbitrary", "arbitrary"),
        name="fox_sample",
    )(page_table.reshape(-1), qkv, qkv, qkv, logf, *([kt] * npp), *([vt] * npp), *([clft] * npp))


def kernel(x_prompt, x_sample, c_prompt, c_sample, state_rwkv, state_rwkv_shift, state_retention, cache_fox_k, cache_fox_v, cache_fox_logf, page_table, w_ada, b_ada, norm_mix_g, norm_ffn_g, final_g, w_in_even, w_out_even, rw_mu, rw_w0, rw_w2, rw_a0, rw_a2, rw_g2, rw_kk, rw_ka, rw_rk, rw_ln_g, rw_ln_b, rw_v0, rw_v1, rw_v2, ret_gn_g, w_in_odd, b_forget, w_out_odd, w_router, b_router, w_gu, b_gu, w_down, b_down):
    bp, seq, _ = x_prompt.shape
    bs, tn_, _ = x_sample.shape
    n_p, n_s = bp * seq, bs * tn_
    npt, nst = n_p // TM, n_s // TM
    depth = w_ada.shape[0]
    n_phys, page = cache_fox_k.shape[1], cache_fox_k.shape[2]
    npg = page_table.shape[1]
    past_len = npg * page
    n_odd = cache_fox_k.shape[0]
    kt_all = jnp.transpose(cache_fox_k, (0, 1, 3, 4, 2)).reshape(n_odd, n_phys, D, page)
    vt_all = jnp.transpose(cache_fox_v, (0, 1, 3, 4, 2)).reshape(n_odd, n_phys, D, page)
    clft_all = jnp.transpose(cache_fox_logf, (0, 1, 3, 2))

    x = jnp.concatenate([x_prompt.reshape(n_p, D), x_sample.reshape(n_s, D)], axis=0)
    ada = _ada(jnp.concatenate([c_prompt, c_sample], axis=0), w_ada, b_ada)

    zero_shift = jnp.zeros((bp, RWKV_PROJ), F32)
    zero_rw = jnp.zeros((bp, A_HEADS, A_HD, A_HD), F32)
    zero_ret = jnp.zeros((bp, B_HEADS, B_HD, B_HD), F32)
    new = {k: [] for k in ('rw_S_p', 'rw_S_s', 'sh_p', 'sh_s', 'ret_p', 'ret_s',
                           'k_p', 'k_s', 'v_p', 'v_s', 'lf_p', 'lf_s')}
    vf_p = vf_s = None
    for i in range(depth):
        modp = ada[i, :, :bp].reshape(6, bp, 1, D)
        mods = jnp.repeat(ada[i, :, bp:], tn_, axis=1)
        both = lambda w, tn, extra=None: (
            _inproj(x, norm_mix_g[i], modp, False, 0, npt, seq, w, tn, extra),
            _inproj(x, norm_mix_g[i], mods, True, npt, nst, seq, w, tn, extra))
        if i % 2 == 0:
            e = i // 2
            w_in = w_in_even[e]
            w_pad = jnp.concatenate([w_in[:, :RWKV_PROJ], jnp.zeros((D, 2048 - RWKV_PROJ), F32),
                                     w_in[:, RWKV_PROJ:]], axis=1).astype(BF16)
            (proj_p,), (proj_s,) = both(w_pad, 2048)
            lp = {'mu': rw_mu[e], 'w0': rw_w0[e], 'w2': rw_w2[e], 'a0': rw_a0[e], 'a2': rw_a2[e],
                  'g2': rw_g2[e], 'k_k': rw_kk[e], 'k_a': rw_ka[e], 'r_k': rw_rk[e],
                  'ln_g': rw_ln_g[e], 'ln_b': rw_ln_b[e]}
            vres = None if e == 0 else (rw_v0[e - 1], rw_v1[e - 1], rw_v2[e - 1])
            oa_p, s_p, sh_p, vf1_p = _rwkv(proj_p, bp, seq, RW_NSEQ, RW_TC, zero_shift, zero_rw, vf_p, lp, vres)
            oa_s, s_s, sh_s, vf1_s = _rwkv(proj_s, bs, tn_, RW_NSEQ, tn_, state_rwkv_shift[e], state_rwkv[e],
                                           vf_s, lp, vres)
            if e == 0:
                vf_p, vf_s = vf1_p, vf1_s
            ob_p, r_p = _retention(proj_p, bp, seq, 1, RET_L, 0, zero_ret, ret_gn_g[e])
            ob_s, r_s = _retention(proj_s, bs, tn_, 8, tn_, past_len, state_retention[e], ret_gn_g[e])
            new['rw_S_p'].append(s_p)
            new['rw_S_s'].append(s_s)
            new['sh_p'].append(sh_p)
            new['sh_s'].append(sh_s)
            new['ret_p'].append(r_p)
            new['ret_s'].append(r_s)
            w_out = w_out_even[e].astype(BF16)
            mixes_p, mixes_s, ws = [oa_p, ob_p], [oa_s, ob_s], [w_out[:A_W], w_out[A_W:]]
        else:
            j = i // 2
            w_in = w_in_odd[j]
            we = jnp.pad(w_in[:, 3 * D:], ((0, 0), (0, LANES - C_HEADS))).astype(BF16)
            be = jnp.pad(b_forget[j], (0, LANES - C_HEADS)).reshape(1, LANES)
            (qkv_p, lf_p), (qkv_s, lf_s) = both(w_in[:, :3 * D].astype(BF16), D, (we, be))
            c = _cumsum_prompt(lf_p, bp, seq)
            ct = jnp.transpose(c[:, :C_HEADS].reshape(bp, seq, C_HEADS), (0, 2, 1))
            o_p = _fox_prompt(qkv_p, c, ct, bp, seq)
            o_s = _fox_sample(qkv_s, lf_s, bs, tn_, page_table, j, kt_all, vt_all, clft_all)
            new['k_p'].append(qkv_p[1].reshape(bp, seq, C_HEADS, C_HD))
            new['k_s'].append(qkv_s[1].reshape(bs, tn_, C_HEADS, C_HD))
            new['v_p'].append(qkv_p[2].reshape(bp, seq, C_HEADS, C_HD))
            new['v_s'].append(qkv_s[2].reshape(bs, tn_, C_HEADS, C_HD))
            new['lf_p'].append(lf_p[:, :C_HEADS].reshape(bp, seq, C_HEADS))
            new['lf_s'].append(lf_s[:, :C_HEADS].reshape(bs, tn_, C_HEADS))
            mixes_p, mixes_s, ws = [o_p], [o_s], [w_out_odd[j].astype(BF16)]
        wr = jnp.pad(w_router[i], ((0, 0), (0, LANES - N_EXPERTS)))
        br = jnp.concatenate([b_router[i], jnp.full((LANES - N_EXPERTS,), NEG, F32)]).reshape(1, LANES)
        x, h, gates, idx, pos, cnt = _outproj(x, mixes_p, mixes_s, ws, modp, mods, norm_ffn_g[i], wr, br, seq)
        counts = cnt[0, :N_EXPERTS].astype(jnp.int32)
        slot_tok, block_exp, nused, slot_of = _moe_meta(idx[:, :TOP_K], pos[:, :TOP_K], counts, MOE_TM)
        y_slots = _moe_experts(h, slot_tok, block_exp, nused, i, w_gu, b_gu, w_down, b_down)
        x = _combine(x, y_slots, slot_of, gates, modp, mods, final_g, seq, i == depth - 1)

    st = lambda name: jnp.stack(new[name])
    return (x[:n_p].reshape(bp, seq, D), x[n_p:].reshape(bs, tn_, D),
            st('rw_S_p'), st('rw_S_s'), st('sh_p'), st('sh_s'), st('ret_p'), st('ret_s'),
            st('k_p'), st('k_s'), st('v_p'), st('v_s'), st('lf_p'), st('lf_s'))
```

```python
import functools
import math

import numpy as np
import jax
import jax.numpy as jnp
from jax import lax
from jax.experimental import pallas as pl
from jax.experimental.pallas import tpu as pltpu

F32 = jnp.float32
BF16 = jnp.bfloat16
HI = lax.Precision.HIGHEST

D = 1024
NORM_EPS = 1e-6
A_W = 512
A_HEADS = 8
A_HD = 64
RWKV_PROJ = 1792
LNX_EPS = 64e-5
B_HEADS = 4
B_HD = 128
GN_EPS = 1e-5
C_HEADS = 16
C_HD = 64
N_EXPERTS = 32
TOP_K = 4
SWIGLU_LIMIT = 7.0
SWIGLU_ALPHA = 1.702

V7X_VMEM_BYTES = 64 * 1024 * 1024
VMEM_LIMIT = V7X_VMEM_BYTES - 8 * 1024 * 1024
LANES = 128
NEG = -1e30

TM = 512
MOE_TM = 512
MOE_LOOKAHEAD = 2
CMB_TM = 256
RW_NSEQ = 8
RW_GROUPS = 2
RW_TC = 64
RET_L = 256
FOX_TQ = 512
FOX_PAGES = 4


def _cp(*sem):
    return pltpu.CompilerParams(dimension_semantics=sem, vmem_limit_bytes=VMEM_LIMIT)


def _sigmoid(x):
    return 1.0 / (1.0 + jnp.exp(-x))


def _softplus(x):
    return jnp.maximum(x, 0.0) + jnp.log(1.0 + jnp.exp(-jnp.abs(x)))


def _bdot(a, b):
    return jnp.dot(a.astype(BF16), b.astype(BF16), preferred_element_type=F32)


def _hdot(a, b):
    return jnp.dot(a, b, precision=HI, preferred_element_type=F32)


def _nt_dot(a, b):
    return lax.dot_general(a, b, (((1,), (1,)), ((), ())), preferred_element_type=F32)


def _rms(x, g):
    return x * lax.rsqrt(jnp.mean(x * x, axis=-1, keepdims=True) + NORM_EPS) * g


def _ada_kernel(c_ref, w_ref, b_ref, o_ref):
    c = c_ref[...]
    o_ref[...] = _bdot(c * _sigmoid(c), w_ref[...]) + b_ref[...]


def _ada(c_all, w_ada, b_ada):
    nl = w_ada.shape[0]
    m = c_all.shape[0]
    return pl.pallas_call(
        _ada_kernel,
        grid=(nl, 6),
        in_specs=[pl.BlockSpec((m, D), lambda l, j: (0, 0)),
                  pl.BlockSpec((None, D, D), lambda l, j: (l, 0, j)),
                  pl.BlockSpec((None, None, 1, D), lambda l, j: (l, j, 0, 0))],
        out_specs=pl.BlockSpec((None, None, m, D), lambda l, j: (l, j, 0, 0)),
        out_shape=jax.ShapeDtypeStruct((nl, 6, m, D), F32),
        compiler_params=_cp("arbitrary", "arbitrary"),
        name="ada",
    )(c_all, w_ada, b_ada.reshape(nl, 6, 1, D))


def _modp_spec(which, tm, seq, nb):
    tps = seq // tm
    return pl.BlockSpec((None, None, 1, D), lambda i, *_: (which, jnp.minimum(i // tps, nb - 1), 0, 0))


def _mods_spec(which, tm, npt):
    return pl.BlockSpec((None, tm, D), lambda i, *_: (which, jnp.maximum(i - npt, 0), 0))


def _inproj_kernel(x_ref, g_ref, sh, sc, w_ref, *rest, has_extra):
    if has_extra:
        we_ref, be_ref, o_ref, oe_ref, h_scr = rest
    else:
        o_ref, h_scr = rest

    @pl.when(pl.program_id(1) == 0)
    def _():
        h = (_rms(x_ref[...], g_ref[...]) * (1.0 + sc[...]) + sh[...]).astype(BF16)
        h_scr[...] = h
        if has_extra:
            f = jnp.dot(h, we_ref[...], preferred_element_type=F32) + be_ref[...]
            oe_ref[...] = jnp.minimum(f, 0.0) - jnp.log(1.0 + jnp.exp(-jnp.abs(f)))

    o_ref[...] = jnp.dot(h_scr[...], w_ref[...], preferred_element_type=F32)


def _inproj(x, g, mod, per_token, tile0, ntiles, seq, w, tn, extra=None):
    nj = w.shape[1] // tn
    rows = ntiles * TM
    has_extra = extra is not None
    if per_token:
        mspec = lambda which: pl.BlockSpec((None, TM, D), lambda i, j: (which, i, 0))
    else:
        tps = seq // TM
        mspec = lambda which: pl.BlockSpec((None, None, 1, D), lambda i, j: (which, i // tps, 0, 0))
    in_specs = [pl.BlockSpec((TM, D), lambda i, j: (tile0 + i, 0)),
                pl.BlockSpec((1, D), lambda i, j: (0, 0)),
                mspec(0), mspec(1),
                pl.BlockSpec((D, tn), lambda i, j: (0, j))]
    args = [x, g.reshape(1, D), mod, mod, w]
    out_specs = [pl.BlockSpec((None, TM, tn), lambda i, j: (j, i, 0))]
    out_shape = [jax.ShapeDtypeStruct((nj, rows, tn), F32)]
    if has_extra:
        we, be = extra
        in_specs += [pl.BlockSpec((D, LANES), lambda i, j: (0, 0)),
                     pl.BlockSpec((1, LANES), lambda i, j: (0, 0))]
        args += [we, be]
        out_specs.append(pl.BlockSpec((TM, LANES), lambda i, j: (i, 0)))
        out_shape.append(jax.ShapeDtypeStruct((rows, LANES), F32))
    return pl.pallas_call(
        functools.partial(_inproj_kernel, has_extra=has_extra),
        grid=(ntiles, nj),
        in_specs=in_specs, out_specs=out_specs, out_shape=out_shape,
        scratch_shapes=[pltpu.VMEM((TM, D), BF16)],
        compiler_params=_cp("arbitrary", "arbitrary"),
        name="inproj",
    )(*args)


def _outproj_kernel(*refs, nparts, npt):
    x_ref = refs[0]
    mixp = refs[1:1 + nparts]
    mixs = refs[1 + nparts:1 + 2 * nparts]
    ws = refs[1 + 2 * nparts:1 + 3 * nparts]
    (gp, gs, nfg, shp, scp, shs, scs, wr, br, xo, ho, go, io, po, co, cnt_s) = refs[1 + 3 * nparts:]
    i = pl.program_id(0)
    isp = i < npt

    @pl.when(i == 0)
    def _():
        cnt_s[...] = jnp.zeros(cnt_s.shape, F32)

    acc = None
    for mp, ms, w in zip(mixp, mixs, ws):
        part = _bdot(jnp.where(isp, mp[...], ms[...]), w[...])
        acc = part if acc is None else acc + part
    xn = x_ref[...] + jnp.where(isp, gp[...], gs[...]) * acc
    xo[...] = xn
    y = _rms(xn, nfg[...])
    h = y * (1.0 + jnp.where(isp, scp[...], scs[...])) + jnp.where(isp, shp[...], shs[...])
    ho[...] = h
    logits = _hdot(h, wr[...]) + br[...]
    tm = logits.shape[0]
    lane = lax.broadcasted_iota(jnp.int32, logits.shape, 1)
    l = logits
    vals, idxs = [], []
    for _ in range(TOP_K):
        m = jnp.max(l, axis=-1, keepdims=True)
        ix = jnp.min(jnp.where(l == m, lane, LANES), axis=-1, keepdims=True)
        vals.append(m)
        idxs.append(ix)
        l = jnp.where(lane == ix, -jnp.inf, l)
    es = [jnp.exp(v - vals[0]) for v in vals]
    den = es[0] + es[1] + es[2] + es[3]
    dense = jnp.zeros(logits.shape, F32)
    for k in range(TOP_K):
        dense = jnp.where(lane == idxs[k], 1.0, dense)
    before = (lax.broadcasted_iota(jnp.int32, (tm, tm), 0) > lax.broadcasted_iota(jnp.int32, (tm, tm), 1))
    rank = jnp.dot(jnp.where(before, 1.0, 0.0).astype(BF16), dense.astype(BF16),
                   preferred_element_type=F32) + cnt_s[...]
    cnt = cnt_s[...] + jnp.sum(dense, axis=0, keepdims=True)
    cnt_s[...] = cnt
    co[...] = cnt
    gates = jnp.zeros(logits.shape, F32)
    idxo = jnp.zeros(logits.shape, jnp.int32)
    poso = jnp.zeros(logits.shape, jnp.int32)
    for k in range(TOP_K):
        pk = jnp.sum(jnp.where(lane == idxs[k], rank, 0.0), axis=-1, keepdims=True)
        gates = jnp.where(lane == k, es[k] / den, gates)
        idxo = jnp.where(lane == k, idxs[k], idxo)
        poso = jnp.where(lane == k, pk.astype(jnp.int32), poso)
    go[...] = gates
    io[...] = idxo
    po[...] = poso


def _outproj(x, mixes_p, mixes_s, ws, modp, mods, nfg, wr, br, seq):
    n = x.shape[0]
    n_s = mods.shape[1]
    npt = (n - n_s) // TM
    nb = modp.shape[1]
    nparts = len(ws)
    row = lambda i: (i, 0)
    const = lambda i: (0, 0)
    in_specs = [pl.BlockSpec((TM, D), row)]
    in_specs += [pl.BlockSpec((TM, m.shape[1]), lambda i: (jnp.minimum(i, npt - 1), 0)) for m in mixes_p]
    in_specs += [pl.BlockSpec((TM, m.shape[1]), lambda i: (jnp.maximum(i - npt, 0), 0)) for m in mixes_s]
    in_specs += [pl.BlockSpec(w.shape, const) for w in ws]
    in_specs += [_modp_spec(2, TM, seq, nb), _mods_spec(2, TM, npt),
                 pl.BlockSpec((1, D), const),
                 _modp_spec(3, TM, seq, nb), _modp_spec(4, TM, seq, nb),
                 _mods_spec(3, TM, npt), _mods_spec(4, TM, npt),
                 pl.BlockSpec((D, LANES), const), pl.BlockSpec((1, LANES), const)]
    args = [x, *mixes_p, *mixes_s, *ws, modp, mods, nfg.reshape(1, D), modp, modp, mods, mods, wr, br]
    return pl.pallas_call(
        functools.partial(_outproj_kernel, nparts=nparts, npt=npt),
        grid=(n // TM,),
        in_specs=in_specs,
        out_specs=[pl.BlockSpec((TM, D), row), pl.BlockSpec((TM, D), row),
                   pl.BlockSpec((TM, LANES), row), pl.BlockSpec((TM, LANES), row),
                   pl.BlockSpec((TM, LANES), row), pl.BlockSpec((1, LANES), const)],
        out_shape=[jax.ShapeDtypeStruct((n, D), F32), jax.ShapeDtypeStruct((n, D), F32),
                   jax.ShapeDtypeStruct((n, LANES), F32), jax.ShapeDtypeStruct((n, LANES), jnp.int32),
                   jax.ShapeDtypeStruct((n, LANES), jnp.int32), jax.ShapeDtypeStruct((1, LANES), F32)],
        scratch_shapes=[pltpu.VMEM((1, LANES), F32)],
        compiler_params=_cp("arbitrary"),
        name="outproj",
    )(*args)


def _moe_kernel(tok_ref, bexp_ref, nused_ref, h_hbm, wgu_ref, bgu_ref, wd_ref, bd_ref,
                o_ref, *rest, tm):
    nbuf = MOE_LOOKAHEAD + 1
    bufs = rest[:nbuf]
    wgu_s, wd_s, sem = rest[nbuf:]
    blk = pl.program_id(0)
    nblk = pl.num_programs(0)
    nu = nused_ref[0]

    def row_copy(b, r, buf, s):
        return pltpu.make_async_copy(h_hbm.at[pl.ds(tok_ref[b * tm + r], 1)], buf.at[pl.ds(r, 1)], sem.at[s])

    def issue_loop(b, buf, s):
        def body(r, c):
            row_copy(b, r, buf, s).start()
            return c
        lax.fori_loop(0, tm, body, 0)

    @pl.when(blk == 0)
    def _():
        for b in range(MOE_LOOKAHEAD):
            issue_loop(b, bufs[b], b)

    def step(cur, cs, nxt, ns):
        pltpu.make_async_copy(h_hbm.at[pl.ds(0, tm)], cur, sem.at[cs]).wait()

        @pl.when(blk < nu)
        def _():
            e = bexp_ref[blk]
            e_prev = bexp_ref[jnp.maximum(blk - 1, 0)]

            @pl.when((blk == 0) | (e != e_prev))
            def _():
                wgu_s[...] = wgu_ref[...].astype(BF16)
                wd_s[...] = wd_ref[...].astype(BF16)

            for r in range(tm):
                row_copy(blk + MOE_LOOKAHEAD, r, nxt, ns).start()
            xb = cur[...].astype(BF16)
            gu = jnp.dot(xb, wgu_s[...], preferred_element_type=F32) + bgu_ref[...]
            gate = jnp.minimum(gu[:, :D], SWIGLU_LIMIT)
            up = jnp.clip(gu[:, D:], -SWIGLU_LIMIT, SWIGLU_LIMIT)
            act = (up + 1.0) * gate * _sigmoid(SWIGLU_ALPHA * gate)
            o_ref[...] = jnp.dot(act.astype(BF16), wd_s[...], preferred_element_type=F32) + bd_ref[...]

        @pl.when(blk >= nu)
        def _():
            @pl.when(blk + MOE_LOOKAHEAD < nblk)
            def _():
                issue_loop(blk + MOE_LOOKAHEAD, nxt, ns)
            o_ref[...] = jnp.zeros(o_ref.shape, F32)

    for k in range(nbuf):
        ahead = (k + MOE_LOOKAHEAD) % nbuf
        pl.when(blk % nbuf == k)(functools.partial(step, bufs[k], k, bufs[ahead], ahead))


def _moe_experts(h, slot_tok, block_exp, nused, layer, w_gu, b_gu, w_down, b_down):
    tm = MOE_TM
    nb = block_exp.shape[0]
    nl = w_gu.shape[0]
    grid_spec = pltpu.PrefetchScalarGridSpec(
        num_scalar_prefetch=3,
        grid=(nb,),
        in_specs=[pl.BlockSpec(memory_space=pl.ANY),
                  pl.BlockSpec((None, None, D, 2 * D), lambda b, t, e, u: (layer, e[b], 0, 0)),
                  pl.BlockSpec((None, None, 1, 2 * D), lambda b, t, e, u: (layer, e[b], 0, 0)),
                  pl.BlockSpec((None, None, D, D), lambda b, t, e, u: (layer, e[b], 0, 0)),
                  pl.BlockSpec((None, None, 1, D), lambda b, t, e, u: (layer, e[b], 0, 0))],
        out_specs=pl.BlockSpec((tm, D), lambda b, t, e, u: (b, 0)),
        scratch_shapes=[pltpu.VMEM((tm, D), F32) for _ in range(MOE_LOOKAHEAD + 1)]
        + [pltpu.VMEM((D, 2 * D), BF16),
           pltpu.VMEM((D, D), BF16),
           pltpu.SemaphoreType.DMA((MOE_LOOKAHEAD + 1,))])
    return pl.pallas_call(
        functools.partial(_moe_kernel, tm=tm),
        grid_spec=grid_spec,
        out_shape=jax.ShapeDtypeStruct((nb * tm, D), F32),
        compiler_params=_cp("arbitrary"),
        name="moe_experts",
    )(slot_tok, block_exp, nused, h, w_gu,
      b_gu.reshape(nl, N_EXPERTS, 1, 2 * D), w_down, b_down.reshape(nl, N_EXPERTS, 1, D))


def _combine_kernel(sof_ref, y_hbm, x_ref, gt_ref, gp, gs, fg_ref, o_ref, ybuf, sem, *, tm, npt, final):
    i = pl.program_id(0)
    nt = pl.num_programs(0)

    def issue(t, slot):
        base = t * (tm * TOP_K)

        def body(r, c):
            for k in range(TOP_K):
                s = sof_ref[base + r * TOP_K + k]
                pltpu.make_async_copy(y_hbm.at[pl.ds(s, 1)], ybuf.at[slot, k, pl.ds(r, 1)], sem.at[slot]).start()
            return c

        lax.fori_loop(0, tm, body, 0, unroll=4)

    @pl.when(i == 0)
    def _():
        issue(0, 0)

    @pl.when(i + 1 < nt)
    def _():
        issue(i + 1, (i + 1) % 2)

    slot = i % 2
    for k in range(TOP_K):
        pltpu.make_async_copy(y_hbm.at[pl.ds(0, tm)], ybuf.at[slot, k], sem.at[slot]).wait()
    gt = gt_ref[...]
    y = ((ybuf[slot, 0] * gt[:, 0:1] + ybuf[slot, 1] * gt[:, 1:2])
         + (ybuf[slot, 2] * gt[:, 2:3] + ybuf[slot, 3] * gt[:, 3:4]))
    xn = x_ref[...] + jnp.where(i < npt, gp[...], gs[...]) * y
    if final:
        xn = _rms(xn, fg_ref[...])
    o_ref[...] = xn


def _combine(x, y_slots, slot_of, gates, modp, mods, final_g, seq, final):
    tm = CMB_TM
    n = x.shape[0]
    n_s = mods.shape[1]
    npt = (n - n_s) // tm
    nb = modp.shape[1]
    grid_spec = pltpu.PrefetchScalarGridSpec(
        num_scalar_prefetch=1,
        grid=(n // tm,),
        in_specs=[pl.BlockSpec(memory_space=pl.ANY),
                  pl.BlockSpec((tm, D), lambda i, s: (i, 0)),
                  pl.BlockSpec((tm, LANES), lambda i, s: (i, 0)),
                  _modp_spec(5, tm, seq, nb), _mods_spec(5, tm, npt),
                  pl.BlockSpec((1, D), lambda i, s: (0, 0))],
        out_specs=pl.BlockSpec((tm, D), lambda i, s: (i, 0)),
        scratch_shapes=[pltpu.VMEM((2, TOP_K, tm, D), F32), pltpu.SemaphoreType.DMA((2,))])
    return pl.pallas_call(
        functools.partial(_combine_kernel, tm=tm, npt=npt, final=final),
        grid_spec=grid_spec,
        out_shape=jax.ShapeDtypeStruct((n, D), F32),
        compiler_params=_cp("arbitrary"),
        name="moe_combine",
    )(slot_of, y_slots, x, gates, modp, mods, final_g.reshape(1, D))


def _moe_meta(idx4, pos4, counts, tm):
    n = idx4.shape[0]
    nk = n * TOP_K
    nb = -(-(nk + N_EXPERTS * (tm - 1)) // tm) + MOE_LOOKAHEAD
    padded = (counts + tm - 1) // tm * tm
    pad_end = jnp.cumsum(padded)
    pad_start = pad_end - padded
    flat = (pad_start[idx4] + pos4).astype(jnp.int32).reshape(nk)
    tok = jnp.repeat(jnp.arange(n, dtype=jnp.int32), TOP_K)
    slot_tok = jnp.zeros((nb * tm,), jnp.int32).at[flat].set(tok)
    starts = jnp.arange(nb, dtype=jnp.int32) * tm
    block_exp = jnp.minimum(jnp.sum((pad_end[None, :] <= starts[:, None]).astype(jnp.int32), axis=1),
                            N_EXPERTS - 1).astype(jnp.int32)
    nused = (pad_end[-1] // tm).astype(jnp.int32).reshape(1)
    return slot_tok, block_exp, nused, flat


def _seg_consts():
    r = np.arange(256)
    mseg = (r[:, None] // A_HD == r[None, :] // A_HD).astype(np.float32)
    c = np.arange(A_W)
    eyet = (np.arange(A_HD)[:, None] == (c[None, :] % A_HD)).astype(np.float32)
    return jnp.asarray(mseg, BF16), jnp.asarray(eyet, F32)


def _split(x):
    hi = x.astype(BF16)
    lo = (x - hi.astype(F32)).astype(BF16)
    return hi, lo


def _segsum(x, mseg):
    hi, lo = _split(x)
    halves = []
    for hf in range(2):
        sl = slice(hf * 256, (hf + 1) * 256)
        halves.append(jnp.dot(hi[:, sl], mseg, preferred_element_type=F32)
                      + jnp.dot(lo[:, sl], mseg, preferred_element_type=F32))
    return jnp.concatenate(halves, axis=1)


def _rwkv_kernel(*refs, nseq, tc, has_vres, ngrp):
    it = iter(refs)
    p_ref, st_ref, s0_ref = next(it), next(it), next(it)
    vf_ref = next(it) if has_vres else None
    mu, w0, w2, a0, a2, g2, kkp, kap, rk, lng, lnb = (next(it) for _ in range(11))
    if has_vres:
        v0, v1, v2 = next(it), next(it), next(it)
    mseg_ref, eyet_ref = next(it), next(it)
    o_ref, sout_ref, shout_ref = next(it), next(it), next(it)
    vfo_ref = None if has_vres else next(it)
    w_s, k_s, v_s, nkk_s, b_s, r_s, out_s, st_s, prev_s = (next(it) for _ in range(9))

    c = pl.program_id(1)
    nch = pl.num_programs(1)
    rows = nseq * tc
    mseg = mseg_ref[...]
    mseg2 = jnp.concatenate([mseg, mseg], axis=0)
    eyet = eyet_ref[...]

    @pl.when(c == 0)
    def _():
        prev_s[...] = st_ref[...]
        for s in range(nseq):
            for h in range(A_HEADS):
                st_s[s, :, h * A_HD:(h + 1) * A_HD] = s0_ref[s, h]

    p = p_ref[:, :, :RWKV_PROJ].reshape(rows, RWKV_PROJ)
    rolled = pltpu.roll(p, 1, 0)
    ridx = lax.broadcasted_iota(jnp.int32, (rows, 1), 0)
    sidx = lax.broadcasted_iota(jnp.int32, (rows, nseq), 1)
    onehot = (lax.broadcasted_iota(jnp.int32, (rows, nseq), 0) == sidx * tc).astype(F32)
    prev = jnp.where(ridx % tc == 0, _hdot(onehot, prev_s[...]), rolled)
    z = p + (prev - p) * mu[...]
    r = z[:, 0:512]
    k = z[:, 512:1024]
    v = z[:, 1024:1536]
    wi = z[:, 1536:1600]
    ai = z[:, 1600:1664]
    gi = z[:, 1664:1792]
    w_log = -_softplus(-(w0[...] + _hdot(jnp.tanh(wi), w2[...]))) - 0.5
    w = jnp.exp(-jnp.exp(w_log))
    a = _sigmoid(a0[...] + _hdot(ai, a2[...]))
    g = _bdot(_sigmoid(gi), g2[...])
    if has_vres:
        vf = vf_ref[...].reshape(rows, A_W)
        v = v + (vf - v) * _sigmoid(v0[...] + _bdot(_bdot(v, v1[...]), v2[...]))
    else:
        vfo_ref[...] = v.reshape(nseq, tc, A_W)
    kk = k * kkp[...]
    kk = kk * lax.rsqrt(jnp.maximum(_segsum(kk * kk, mseg), 1e-24))
    k = k * (1.0 + (a - 1.0) * kap[...])
    w_s[...] = w
    k_s[...] = k
    v_s[...] = v
    nkk_s[...] = -kk
    b_s[...] = kk * a
    r_s[...] = r
    bonus = _segsum(r * k * rk[...], mseg) * v

    def halves(x):
        return [x[:, 0:256], x[:, 256:512]]

    def token(t, carry):
        groups = [range(g0, g0 + nseq // ngrp) for g0 in range(0, nseq, nseq // ngrp)]
        ress = []
        for grp in groups:
            lhs, lhv = [], []
            for s in grp:
                row = s * tc + t
                x1h, x1l = _split(st_s[s] * nkk_s[pl.ds(row, 1), :])
                lhs += [jnp.concatenate([a, b], axis=1) for a, b in zip(halves(x1h), halves(x1l))]
                lhv += halves((eyet * v_s[pl.ds(row, 1), :]).astype(BF16))
            ress.append((jnp.dot(jnp.concatenate(lhs, axis=0), mseg2, preferred_element_type=F32),
                         jnp.dot(jnp.concatenate(lhv, axis=0), mseg, preferred_element_type=F32)))
        ress2 = []
        for grp, (res, resv) in zip(groups, ress):
            lhs2 = []
            for n, s in enumerate(grp):
                row = s * tc + t
                o = n * 2 * A_HD
                sa = jnp.concatenate([res[o:o + A_HD], res[o + A_HD:o + 2 * A_HD]], axis=1)
                v2 = jnp.concatenate([resv[o:o + A_HD], resv[o + A_HD:o + 2 * A_HD]], axis=1)
                st = (st_s[s] * w_s[pl.ds(row, 1), :] + sa * b_s[pl.ds(row, 1), :]
                      + v2 * k_s[pl.ds(row, 1), :])
                st_s[s] = st
                lhs2 += halves((st * r_s[pl.ds(row, 1), :]).astype(BF16))
            ress2.append(jnp.dot(jnp.concatenate(lhs2, axis=0), mseg, preferred_element_type=F32))
        for grp, res2 in zip(groups, ress2):
            for n, s in enumerate(grp):
                row = s * tc + t
                o = n * 2 * A_HD
                o2 = jnp.concatenate([res2[o:o + A_HD], res2[o + A_HD:o + 2 * A_HD]], axis=1)
                out_s[pl.ds(row, 1), :] = jnp.sum(o2 * eyet, axis=0, keepdims=True)
        return carry

    lax.fori_loop(0, tc, token, 0)

    out = out_s[...]
    xc = out - _segsum(out, mseg) * (1.0 / A_HD)
    var = _segsum(xc * xc, mseg) * (1.0 / A_HD)
    y = xc * lax.rsqrt(var + LNX_EPS) * lng[...] + lnb[...]
    o_ref[...] = ((y + bonus) * g).reshape(nseq, tc, A_W)

    for s in range(nseq):
        prev_s[s:s + 1, :] = p[s * tc + tc - 1:s * tc + tc, :]

    @pl.when(c == nch - 1)
    def _():
        shout_ref[...] = prev_s[...]
        for s in range(nseq):
            for h in range(A_HEADS):
                sout_ref[s, h] = st_s[s, :, h * A_HD:(h + 1) * A_HD]


def _rwkv(proj, nb, seq, nseq, tc, st_shift, s0, vfirst, lp, vres):
    nch = seq // tc
    rows = nseq * tc
    has_vres = vres is not None
    blk3 = lambda b, c: (b, c, 0)
    vec = lambda a: a.reshape(1, -1)
    c2 = lambda b, c: (0, 0)
    mseg, eyet = _seg_consts()
    in_specs = [pl.BlockSpec((None, nseq, tc, 2048), lambda b, c: (0, b, c, 0)),
                pl.BlockSpec((None, nseq, RWKV_PROJ), lambda b, c: (b, 0, 0)),
                pl.BlockSpec((nseq, A_HEADS, A_HD, A_HD), lambda b, c: (b, 0, 0, 0))]
    args = [proj.reshape(proj.shape[0], nb, seq, 2048), st_shift.reshape(nb // nseq, nseq, RWKV_PROJ), s0]
    if has_vres:
        in_specs.append(pl.BlockSpec((nseq, tc, A_W), blk3))
        args.append(vfirst)
    small = [vec(lp['mu']), vec(lp['w0']), lp['w2'], vec(lp['a0']), lp['a2'], lp['g2'].astype(BF16),
             vec(lp['k_k']), vec(lp['k_a']), vec(lp['r_k']), vec(lp['ln_g']), vec(lp['ln_b'])]
    if has_vres:
        small += [vec(vres[0]), vres[1].astype(BF16), vres[2].astype(BF16)]
    small += [mseg, eyet]
    in_specs += [pl.BlockSpec(a.shape, c2) for a in small]
    args += small
    out_specs = [pl.BlockSpec((nseq, tc, A_W), blk3),
                 pl.BlockSpec((nseq, A_HEADS, A_HD, A_HD), lambda b, c: (b, 0, 0, 0)),
                 pl.BlockSpec((None, nseq, RWKV_PROJ), lambda b, c: (b, 0, 0))]
    out_shape = [jax.ShapeDtypeStruct((nb, seq, A_W), F32),
                 jax.ShapeDtypeStruct((nb, A_HEADS, A_HD, A_HD), F32),
                 jax.ShapeDtypeStruct((nb // nseq, nseq, RWKV_PROJ), F32)]
    if not has_vres:
        out_specs.append(pl.BlockSpec((nseq, tc, A_W), blk3))
        out_shape.append(jax.ShapeDtypeStruct((nb, seq, A_W), F32))
    scratch = [pltpu.VMEM((rows, A_W), F32) for _ in range(7)]
    scratch += [pltpu.VMEM((nseq, A_HD, A_W), F32), pltpu.VMEM((nseq, RWKV_PROJ), F32)]
    outs = pl.pallas_call(
        functools.partial(_rwkv_kernel, nseq=nseq, tc=tc, has_vres=has_vres, ngrp=RW_GROUPS),
        grid=(nb // nseq, nch),
        in_specs=in_specs, out_specs=out_specs, out_shape=out_shape,
        scratch_shapes=scratch,
        compiler_params=_cp("arbitrary", "arbitrary"),
        name="rwkv",
    )(*args)
    o, s_out, sh_out = outs[0].reshape(nb * seq, A_W), outs[1], outs[2].reshape(nb, RWKV_PROJ)
    vf_out = vfirst if has_vres else outs[3]
    return o, s_out, sh_out, vf_out


def _ret_kernel(pr_ref, s0_ref, cos_ref, sin_ref, gn_ref, o_ref, sout_ref, st_s, *, nseq, L):
    c = pl.program_id(1)
    nch = pl.num_programs(1)

    @pl.when(c == 0)
    def _():
        st_s[...] = s0_ref[...]

    cos = cos_ref[...]
    sin = sin_ref[...]
    ii = lax.broadcasted_iota(jnp.int32, (L, L), 0)
    jj = lax.broadcasted_iota(jnp.int32, (L, L), 1)
    dif = (ii - jj).astype(F32)
    ri = lax.broadcasted_iota(jnp.int32, (L, 1), 0).astype(F32)

    def rot(x):
        return x * cos + pltpu.roll(x, B_HD // 2, 1) * sin

    for h in range(B_HEADS):
        lg = math.log1p(-2.0 ** (-5.0 - h))
        intra = jnp.where(dif >= 0, jnp.exp(jnp.maximum(dif, 0.0) * lg), 0.0)
        q_dec = jnp.exp((ri + 1.0) * lg)
        k_dec = jnp.exp((L - 1.0 - ri) * lg)
        c_dec = math.exp(L * lg)
        gn = gn_ref[:, h * B_HD:(h + 1) * B_HD]
        for s in range(nseq):
            rs = slice(s * L, (s + 1) * L)
            q = rot(pr_ref[rs, h * B_HD:(h + 1) * B_HD])
            k = rot(pr_ref[rs, 512 + h * B_HD:512 + (h + 1) * B_HD]) * (B_HD ** -0.5)
            v = pr_ref[rs, 1024 + h * B_HD:1024 + (h + 1) * B_HD]
            g = pr_ref[rs, 1536 + h * B_HD:1536 + (h + 1) * B_HD]
            st = st_s[s, h]
            vb = v.astype(BF16)
            att = _nt_dot(q.astype(BF16), k.astype(BF16)) * intra
            o = (jnp.dot(att.astype(BF16), vb, preferred_element_type=F32)
                 + _bdot(q * q_dec, st))
            st_s[s, h] = st * c_dec + lax.dot_general(
                (k * k_dec).astype(BF16), vb, (((0,), (0,)), ((), ())), preferred_element_type=F32)
            oc = o - jnp.mean(o, axis=-1, keepdims=True)
            y = oc * lax.rsqrt(jnp.mean(oc * oc, axis=-1, keepdims=True) + GN_EPS) * gn
            o_ref[rs, h * B_HD:(h + 1) * B_HD] = g * _sigmoid(g) * y

    @pl.when(c == nch - 1)
    def _():
        sout_ref[...] = st_s[...]


def _rot_tables(pos0, t):
    half = B_HD // 2
    inv = 1.0 / (10000.0 ** jnp.linspace(0.0, 1.0, half, dtype=F32))
    ang = (pos0 + jnp.arange(t, dtype=F32))[:, None] * inv[None, :]
    cos, sin = jnp.cos(ang), jnp.sin(ang)
    return jnp.concatenate([cos, cos], axis=1), jnp.concatenate([-sin, sin], axis=1)


def _retention(proj, nb, seq, nseq, L, pos0, s0, gn_g):
    n = proj.shape[1]
    nch = seq // L
    rows = nseq * L
    cos, sin = _rot_tables(float(pos0), seq)
    rowmap = lambda b, c: (b * nch + c, 0)
    return pl.pallas_call(
        functools.partial(_ret_kernel, nseq=nseq, L=L),
        grid=(nb // nseq, nch),
        in_specs=[pl.BlockSpec((None, rows, 2048), lambda b, c: (1, b * nch + c, 0)),
                  pl.BlockSpec((nseq, B_HEADS, B_HD, B_HD), lambda b, c: (b, 0, 0, 0)),
                  pl.BlockSpec((L, B_HD), lambda b, c: (c, 0)),
                  pl.BlockSpec((L, B_HD), lambda b, c: (c, 0)),
                  pl.BlockSpec((1, 512), lambda b, c: (0, 0))],
        out_specs=[pl.BlockSpec((rows, 512), rowmap),
                   pl.BlockSpec((nseq, B_HEADS, B_HD, B_HD), lambda b, c: (b, 0, 0, 0))],
        out_shape=[jax.ShapeDtypeStruct((n, 512), F32),
                   jax.ShapeDtypeStruct((nb, B_HEADS, B_HD, B_HD), F32)],
        scratch_shapes=[pltpu.VMEM((nseq, B_HEADS, B_HD, B_HD), F32)],
        compiler_params=_cp("arbitrary", "arbitrary"),
        name="retention",
    )(proj, s0, cos, sin, gn_g.reshape(1, 512))


def _tri(n):
    return (lax.broadcasted_iota(jnp.int32, (n, n), 0) >= lax.broadcasted_iota(jnp.int32, (n, n), 1)).astype(F32)


def _cumsum_kernel(lf_ref, o_ref):
    t = lf_ref.shape[0]
    tri = _tri(LANES)
    carry = jnp.zeros((1, LANES), F32)
    for b in range(t // LANES):
        rs = slice(b * LANES, (b + 1) * LANES)
        cb = _hdot(tri, lf_ref[rs, :]) + carry
        o_ref[rs, :] = cb
        carry = cb[LANES - 1:LANES, :]


def _cumsum_prompt(logf, nb, seq):
    return pl.pallas_call(
        _cumsum_kernel,
        grid=(nb,),
        in_specs=[pl.BlockSpec((seq, LANES), lambda b: (b, 0))],
        out_specs=pl.BlockSpec((seq, LANES), lambda b: (b, 0)),
        out_shape=jax.ShapeDtypeStruct((nb * seq, LANES), F32),
        compiler_params=_cp("arbitrary"),
        name="fox_cumsum",
    )(logf)


def _foxp_kernel(q_ref, k_ref, v_ref, cq_ref, ck_ref, o_ref, *, tq):
    hp = pl.program_id(1)
    qi = pl.program_id(2)
    lane = lax.broadcasted_iota(jnp.int32, (tq, LANES), 1)
    qs, cqs = [], []
    for hh in range(2):
        hs = slice(hh * C_HD, (hh + 1) * C_HD)
        qs.append((q_ref[:, hs] * (C_HD ** -0.5)).astype(BF16))
        cqs.append(jnp.sum(jnp.where(lane == hp * 2 + hh, cq_ref[...], 0.0), axis=-1, keepdims=True))

    def block(off, diag, carry):
        new = []
        for hh in range(2):
            m, l, acc = carry[3 * hh:3 * hh + 3]
            hs = slice(hh * C_HD, (hh + 1) * C_HD)
            k = k_ref[pl.ds(off, tq), hs].astype(BF16)
            v = v_ref[pl.ds(off, tq), hs].astype(BF16)
            s = _nt_dot(qs[hh], k) + (cqs[hh] - ck_ref[pl.ds(hp * 2 + hh, 1), pl.ds(off, tq)])
            if diag:
                s = jnp.where(lax.broadcasted_iota(jnp.int32, (tq, tq), 0)
                              >= lax.broadcasted_iota(jnp.int32, (tq, tq), 1), s, NEG)
            m_new = jnp.maximum(m, jnp.max(s, axis=-1, keepdims=True))
            alpha = jnp.exp(m - m_new)
            p = jnp.exp(s - m_new)
            new += [m_new, alpha * l + jnp.sum(p, axis=-1, keepdims=True),
                    alpha * acc + jnp.dot(p.astype(BF16), v, preferred_element_type=F32)]
        return tuple(new)

    init = (jnp.full((tq, 1), NEG, F32), jnp.zeros((tq, 1), F32), jnp.zeros((tq, C_HD), F32)) * 2
    carry = lax.fori_loop(0, qi, lambda kj, c: block(pl.multiple_of(kj * tq, tq), False, c), init)
    carry = block(pl.multiple_of(qi * tq, tq), True, carry)
    o_ref[...] = jnp.concatenate([carry[2] / carry[1], carry[5] / carry[4]], axis=1)


def _fox_prompt(qkv, c, ct, nb, seq):
    n = qkv.shape[1]
    tq = FOX_TQ
    nq = seq // tq
    return pl.pallas_call(
        functools.partial(_foxp_kernel, tq=tq),
        grid=(nb, C_HEADS // 2, nq),
        in_specs=[pl.BlockSpec((None, tq, LANES), lambda b, hp, qi: (0, b * nq + qi, hp)),
                  pl.BlockSpec((None, seq, LANES), lambda b, hp, qi: (1, b, hp)),
                  pl.BlockSpec((None, seq, LANES), lambda b, hp, qi: (2, b, hp)),
                  pl.BlockSpec((tq, LANES), lambda b, hp, qi: (b * nq + qi, 0)),
                  pl.BlockSpec((None, C_HEADS, seq), lambda b, hp, qi: (b, 0, 0))],
        out_specs=pl.BlockSpec((tq, LANES), lambda b, hp, qi: (b * nq + qi, hp)),
        out_shape=jax.ShapeDtypeStruct((n, D), F32),
        compiler_params=_cp("arbitrary", "arbitrary", "arbitrary"),
        name="fox_prompt",
    )(qkv, qkv, qkv, c, ct)


def _foxs_kernel(pt_ref, q_ref, kn_ref, vn_ref, lfn_ref, *rest, t_new, npp):
    kts, vts, clfs = rest[0:npp], rest[npp:2 * npp], rest[2 * npp:3 * npp]
    o_ref, qbd, cnq_s, m_s, l_s, acc_s, carry = rest[3 * npp:]
    g = pl.program_id(1)
    nrow = C_HEADS * t_new
    page = clfs[0].shape[1]
    rowh = lax.broadcasted_iota(jnp.int32, (nrow, D), 0) // t_new
    bmask = (lax.broadcasted_iota(jnp.int32, (nrow, D), 1) // C_HD) == rowh

    def per_row(x):
        return jnp.concatenate([jnp.broadcast_to(x[h:h + 1, :], (t_new, x.shape[1])) for h in range(C_HEADS)],
                               axis=0)

    def update(s, pv):
        m = m_s[...]
        m_new = jnp.maximum(m, jnp.max(s, axis=-1, keepdims=True))
        alpha = jnp.exp(m - m_new)
        p = jnp.exp(s - m_new)
        l_s[...] = alpha * l_s[...] + jnp.sum(p, axis=-1, keepdims=True)
        acc_s[...] = alpha * acc_s[...] + pv(p)
        m_s[...] = m_new

    @pl.when(g == 0)
    def _():
        q = q_ref[...] * (C_HD ** -0.5)
        qbd[...] = jnp.where(bmask, jnp.concatenate([q] * C_HEADS, axis=0), 0.0).astype(BF16)
        m_s[...] = jnp.full(m_s.shape, NEG, F32)
        l_s[...] = jnp.zeros(l_s.shape, F32)
        acc_s[...] = jnp.zeros(acc_s.shape, F32)
        carry[...] = jnp.zeros(carry.shape, F32)
        tle = (lax.broadcasted_iota(jnp.int32, (t_new, t_new), 0)
               <= lax.broadcasted_iota(jnp.int32, (t_new, t_new), 1)).astype(F32)
        cnt = lax.dot_general(lfn_ref[:, :C_HEADS], tle, (((0,), (0,)), ((), ())),
                              precision=HI, preferred_element_type=F32)
        gk = per_row(cnt)
        colt = lax.broadcasted_iota(jnp.int32, (nrow, t_new), 1)
        qrow = lax.broadcasted_iota(jnp.int32, (nrow, t_new), 0) % t_new
        cnq = jnp.sum(jnp.where(colt == qrow, gk, 0.0), axis=-1, keepdims=True)
        cnq_s[...] = cnq
        s2 = _nt_dot(qbd[...], kn_ref[...].astype(BF16)) + cnq - gk
        update(jnp.where(colt <= qrow, s2, NEG),
               lambda p: jnp.dot(p, vn_ref[...], preferred_element_type=F32))

    later = (lax.broadcasted_iota(jnp.int32, (page, page), 0)
             > lax.broadcasted_iota(jnp.int32, (page, page), 1)).astype(F32)
    lfs = [r[...] for r in clfs]
    suf = _hdot(jnp.concatenate(lfs, axis=0), later)
    run = carry[...]
    ds = []
    for i in range(npp):
        si = suf[i * C_HEADS:(i + 1) * C_HEADS]
        ds.append(si + run)
        run = run + si[:, 0:1] + lfs[i][:, 0:1]
    carry[...] = run
    qb = qbd[...]
    s = jnp.concatenate([jnp.dot(qb, kt[...].astype(BF16), preferred_element_type=F32) for kt in kts], axis=1)
    s = s + (per_row(jnp.concatenate(ds, axis=1)) + cnq_s[...])

    def pv(p):
        pb = p.astype(BF16)
        out = _nt_dot(pb[:, 0:page], vts[0][...].astype(BF16))
        for i in range(1, npp):
            out = out + _nt_dot(pb[:, i * page:(i + 1) * page], vts[i][...].astype(BF16))
        return out

    update(s, pv)

    @pl.when(g == pl.num_programs(1) - 1)
    def _():
        o = jnp.where(bmask, acc_s[...] / l_s[...], 0.0)
        o_ref[...] = jnp.sum(o.reshape(C_HEADS, t_new, D), axis=0)


def _fox_sample(qkv, logf, nb, t_new, page_table, layer, kt, vt, clft):
    n = qkv.shape[1]
    npg = page_table.shape[1]
    page = kt.shape[3]
    nrow = C_HEADS * t_new
    npp = FOX_PAGES

    def pidx(i):
        return lambda b, g, pt: (layer, pt[b * npg + (npg - 1 - (g * npp + i))], 0, 0)

    in_specs = [pl.BlockSpec((None, t_new, D), lambda b, g, pt: (0, b, 0)),
                pl.BlockSpec((None, t_new, D), lambda b, g, pt: (1, b, 0)),
                pl.BlockSpec((None, t_new, D), lambda b, g, pt: (2, b, 0)),
                pl.BlockSpec((t_new, LANES), lambda b, g, pt: (b, 0))]
    in_specs += [pl.BlockSpec((None, None, D, page), pidx(i)) for i in range(npp)]
    in_specs += [pl.BlockSpec((None, None, D, page), pidx(i)) for i in range(npp)]
    in_specs += [pl.BlockSpec((None, None, C_HEADS, page), pidx(i)) for i in range(npp)]
    grid_spec = pltpu.PrefetchScalarGridSpec(
        num_scalar_prefetch=1,
        grid=(nb, npg // npp),
        in_specs=in_specs,
        out_specs=pl.BlockSpec((t_new, D), lambda b, g, pt: (b, 0)),
        scratch_shapes=[pltpu.VMEM((nrow, D), BF16), pltpu.VMEM((nrow, 1), F32), pltpu.VMEM((nrow, 1), F32),
                        pltpu.VMEM((nrow, 1), F32), pltpu.VMEM((nrow, D), F32), pltpu.VMEM((C_HEADS, 1), F32)])
    return pl.pallas_call(
        functools.partial(_foxs_kernel, t_new=t_new, npp=npp),
        grid_spec=grid_spec,
        out_shape=jax.ShapeDtypeStruct((n, D), F32),
        compiler_params=_cp("arbitrary", "arbitrary"),
        name="fox_sample",
    )(page_table.reshape(-1), qkv, qkv, qkv, logf, *([kt] * npp), *([vt] * npp), *([clft] * npp))


def kernel(x_prompt, x_sample, c_prompt, c_sample, state_rwkv, state_rwkv_shift, state_retention, cache_fox_k, cache_fox_v, cache_fox_logf, page_table, w_ada, b_ada, norm_mix_g, norm_ffn_g, final_g, w_in_even, w_out_even, rw_mu, rw_w0, rw_w2, rw_a0, rw_a2, rw_g2, rw_kk, rw_ka, rw_rk, rw_ln_g, rw_ln_b, rw_v0, rw_v1, rw_v2, ret_gn_g, w_in_odd, b_forget, w_out_odd, w_router, b_router, w_gu, b_gu, w_down, b_down):
    bp, seq, _ = x_prompt.shape
    bs, tn_, _ = x_sample.shape
    n_p, n_s = bp * seq, bs * tn_
    npt, nst = n_p // TM, n_s // TM
    depth = w_ada.shape[0]
    n_phys, page = cache_fox_k.shape[1], cache_fox_k.shape[2]
    npg = page_table.shape[1]
    past_len = npg * page
    n_odd = cache_fox_k.shape[0]
    kt_all = jnp.transpose(cache_fox_k, (0, 1, 3, 4, 2)).reshape(n_odd, n_phys, D, page)
    vt_all = jnp.transpose(cache_fox_v, (0, 1, 3, 4, 2)).reshape(n_odd, n_phys, D, page)
    clft_all = jnp.transpose(cache_fox_logf, (0, 1, 3, 2))

    x = jnp.concatenate([x_prompt.reshape(n_p, D), x_sample.reshape(n_s, D)], axis=0)
    ada = _ada(jnp.concatenate([c_prompt, c_sample], axis=0), w_ada, b_ada)

    zero_shift = jnp.zeros((bp, RWKV_PROJ), F32)
    zero_rw = jnp.zeros((bp, A_HEADS, A_HD, A_HD), F32)
    zero_ret = jnp.zeros((bp, B_HEADS, B_HD, B_HD), F32)
    new = {k: [] for k in ('rw_S_p', 'rw_S_s', 'sh_p', 'sh_s', 'ret_p', 'ret_s',
                           'k_p', 'k_s', 'v_p', 'v_s', 'lf_p', 'lf_s')}
    vf_p = vf_s = None
    for i in range(depth):
        modp = ada[i, :, :bp].reshape(6, bp, 1, D)
        mods = jnp.repeat(ada[i, :, bp:], tn_, axis=1)
        both = lambda w, tn, extra=None: (
            _inproj(x, norm_mix_g[i], modp, False, 0, npt, seq, w, tn, extra),
            _inproj(x, norm_mix_g[i], mods, True, npt, nst, seq, w, tn, extra))
        if i % 2 == 0:
            e = i // 2
            w_in = w_in_even[e]
            w_pad = jnp.concatenate([w_in[:, :RWKV_PROJ], jnp.zeros((D, 2048 - RWKV_PROJ), F32),
                                     w_in[:, RWKV_PROJ:]], axis=1).astype(BF16)
            (proj_p,), (proj_s,) = both(w_pad, 2048)
            lp = {'mu': rw_mu[e], 'w0': rw_w0[e], 'w2': rw_w2[e], 'a0': rw_a0[e], 'a2': rw_a2[e],
                  'g2': rw_g2[e], 'k_k': rw_kk[e], 'k_a': rw_ka[e], 'r_k': rw_rk[e],
                  'ln_g': rw_ln_g[e], 'ln_b': rw_ln_b[e]}
            vres = None if e == 0 else (rw_v0[e - 1], rw_v1[e - 1], rw_v2[e - 1])
            oa_p, s_p, sh_p, vf1_p = _rwkv(proj_p, bp, seq, RW_NSEQ, RW_TC, zero_shift, zero_rw, vf_p, lp, vres)
            oa_s, s_s, sh_s, vf1_s = _rwkv(proj_s, bs, tn_, RW_NSEQ, tn_, state_rwkv_shift[e], state_rwkv[e],
                                           vf_s, lp, vres)
            if e == 0:
                vf_p, vf_s = vf1_p, vf1_s
            ob_p, r_p = _retention(proj_p, bp, seq, 1, RET_L, 0, zero_ret, ret_gn_g[e])
            ob_s, r_s = _retention(proj_s, bs, tn_, 8, tn_, past_len, state_retention[e], ret_gn_g[e])
            new['rw_S_p'].append(s_p)
            new['rw_S_s'].append(s_s)
            new['sh_p'].append(sh_p)
            new['sh_s'].append(sh_s)
            new['ret_p'].append(r_p)
            new['ret_s'].append(r_s)
            w_out = w_out_even[e].astype(BF16)
            mixes_p, mixes_s, ws = [oa_p, ob_p], [oa_s, ob_s], [w_out[:A_W], w_out[A_W:]]
        else:
            j = i // 2
            w_in = w_in_odd[j]
            we = jnp.pad(w_in[:, 3 * D:], ((0, 0), (0, LANES - C_HEADS))).astype(BF16)
            be = jnp.pad(b_forget[j], (0, LANES - C_HEADS)).reshape(1, LANES)
            (qkv_p, lf_p), (qkv_s, lf_s) = both(w_in[:, :3 * D].astype(BF16), D, (we, be))
            c = _cumsum_prompt(lf_p, bp, seq)
            ct = jnp.transpose(c[:, :C_HEADS].reshape(bp, seq, C_HEADS), (0, 2, 1))
            o_p = _fox_prompt(qkv_p, c, ct, bp, seq)
            o_s = _fox_sample(qkv_s, lf_s, bs, tn_, page_table, j, kt_all, vt_all, clft_all)
            new['k_p'].append(qkv_p[1].reshape(bp, seq, C_HEADS, C_HD))
            new['k_s'].append(qkv_s[1].reshape(bs, tn_, C_HEADS, C_HD))
            new['v_p'].append(qkv_p[2].reshape(bp, seq, C_HEADS, C_HD))
            new['v_s'].append(qkv_s[2].reshape(bs, tn_, C_HEADS, C_HD))
            new['lf_p'].append(lf_p[:, :C_HEADS].reshape(bp, seq, C_HEADS))
            new['lf_s'].append(lf_s[:, :C_HEADS].reshape(bs, tn_, C_HEADS))
            mixes_p, mixes_s, ws = [o_p], [o_s], [w_out_odd[j].astype(BF16)]
        wr = jnp.pad(w_router[i], ((0, 0), (0, LANES - N_EXPERTS)))
        br = jnp.concatenate([b_router[i], jnp.full((LANES - N_EXPERTS,), NEG, F32)]).reshape(1, LANES)
        x, h, gates, idx, pos, cnt = _outproj(x, mixes_p, mixes_s, ws, modp, mods, norm_ffn_g[i], wr, br, seq)
        counts = cnt[0, :N_EXPERTS].astype(jnp.int32)
        slot_tok, block_exp, nused, slot_of = _moe_meta(idx[:, :TOP_K], pos[:, :TOP_K], counts, MOE_TM)
        y_slots = _moe_experts(h, slot_tok, block_exp, nused, i, w_gu, b_gu, w_down, b_down)
        x = _combine(x, y_slots, slot_of, gates, modp, mods, final_g, seq, i == depth - 1)

    st = lambda name: jnp.stack(new[name])
    return (x[:n_p].reshape(bp, seq, D), x[n_p:].reshape(bs, tn_, D),
            st('rw_S_p'), st('rw_S_s'), st('sh_p'), st('sh_s'), st('ret_p'), st('ret_s'),
            st('k_p'), st('k_s'), st('v_p'), st('v_s'), st('lf_p'), st('lf_s'))
```

```python
import functools
import math

import numpy as np
import jax
import jax.numpy as jnp
from jax import lax
from jax.experimental import pallas as pl
from jax.experimental.pallas import tpu as pltpu

F32 = jnp.float32
BF16 = jnp.bfloat16
HI = lax.Precision.HIGHEST

D = 1024
NORM_EPS = 1e-6
A_W = 512
A_HEADS = 8
A_HD = 64
RWKV_PROJ = 1792
LNX_EPS = 64e-5
B_HEADS = 4
B_HD = 128
GN_EPS = 1e-5
C_HEADS = 16
C_HD = 64
N_EXPERTS = 32
TOP_K = 4
SWIGLU_LIMIT = 7.0
SWIGLU_ALPHA = 1.702

V7X_VMEM_BYTES = 64 * 1024 * 1024
VMEM_LIMIT = V7X_VMEM_BYTES - 8 * 1024 * 1024
LANES = 128
NEG = -1e30

TM = 512
MOE_TM = 512
MOE_LOOKAHEAD = 2
CMB_TM = 256
CMB_UNROLL = 8
RW_NSEQ = 8
RW_GROUPS = 2
RW_TC = 64
RET_L = 256
FOX_TQ = 512
FOX_PAGES = 8


def _cp(*sem):
    return pltpu.CompilerParams(dimension_semantics=sem, vmem_limit_bytes=VMEM_LIMIT)


def _sigmoid(x):
    return 1.0 / (1.0 + jnp.exp(-x))


def _softplus(x):
    return jnp.maximum(x, 0.0) + jnp.log(1.0 + jnp.exp(-jnp.abs(x)))


def _bdot(a, b):
    return jnp.dot(a.astype(BF16), b.astype(BF16), preferred_element_type=F32)


def _hdot(a, b):
    return jnp.dot(a, b, precision=HI, preferred_element_type=F32)


def _nt_dot(a, b):
    return lax.dot_general(a, b, (((1,), (1,)), ((), ())), preferred_element_type=F32)


def _rms(x, g):
    return x * lax.rsqrt(jnp.mean(x * x, axis=-1, keepdims=True) + NORM_EPS) * g


def _store_tiles(ref, x):
    for s in range(x.shape[1] // LANES):
        ref[:, s, :] = x[:, s * LANES:(s + 1) * LANES]


def _load_tiles(ref):
    return jnp.concatenate([ref[:, s, :] for s in range(ref.shape[1])], axis=1)


def _ada_kernel(c_ref, w_ref, b_ref, o_ref):
    c = c_ref[...]
    o_ref[...] = _bdot(c * _sigmoid(c), w_ref[...]) + b_ref[...]


def _ada(c_all, w_ada, b_ada):
    nl = w_ada.shape[0]
    m = c_all.shape[0]
    return pl.pallas_call(
        _ada_kernel,
        grid=(nl, 6),
        in_specs=[pl.BlockSpec((m, D), lambda l, j: (0, 0)),
                  pl.BlockSpec((None, D, D), lambda l, j: (l, 0, j)),
                  pl.BlockSpec((None, None, 1, D), lambda l, j: (l, j, 0, 0))],
        out_specs=pl.BlockSpec((None, None, m, D), lambda l, j: (l, j, 0, 0)),
        out_shape=jax.ShapeDtypeStruct((nl, 6, m, D), F32),
        compiler_params=_cp("arbitrary", "arbitrary"),
        name="ada",
    )(c_all, w_ada, b_ada.reshape(nl, 6, 1, D))


def _modp_spec(which, tm, seq, nb):
    tps = seq // tm
    return pl.BlockSpec((None, None, 1, D), lambda i, *_: (which, jnp.minimum(i // tps, nb - 1), 0, 0))


def _mods_spec(which, tm, npt):
    return pl.BlockSpec((None, tm, D), lambda i, *_: (which, jnp.maximum(i - npt, 0), 0))


def _inproj_kernel(x_ref, g_ref, sh, sc, w_ref, *rest, has_extra):
    if has_extra:
        we_ref, be_ref, o_ref, oe_ref, h_scr = rest
    else:
        o_ref, h_scr = rest

    @pl.when(pl.program_id(1) == 0)
    def _():
        h = (_rms(x_ref[...], g_ref[...]) * (1.0 + sc[...]) + sh[...]).astype(BF16)
        h_scr[...] = h
        if has_extra:
            f = jnp.dot(h, we_ref[...], preferred_element_type=F32) + be_ref[...]
            oe_ref[...] = jnp.minimum(f, 0.0) - jnp.log(1.0 + jnp.exp(-jnp.abs(f)))

    o_ref[...] = jnp.dot(h_scr[...], w_ref[...], preferred_element_type=F32)


def _inproj(x, g, mod, per_token, tile0, ntiles, seq, w, tn, extra=None):
    nj = w.shape[1] // tn
    rows = ntiles * TM
    has_extra = extra is not None
    if per_token:
        mspec = lambda which: pl.BlockSpec((None, TM, D), lambda i, j: (which, i, 0))
    else:
        tps = seq // TM
        mspec = lambda which: pl.BlockSpec((None, None, 1, D), lambda i, j: (which, i // tps, 0, 0))
    in_specs = [pl.BlockSpec((TM, D), lambda i, j: (tile0 + i, 0)),
                pl.BlockSpec((1, D), lambda i, j: (0, 0)),
                mspec(0), mspec(1),
                pl.BlockSpec((D, tn), lambda i, j: (0, j))]
    args = [x, g.reshape(1, D), mod, mod, w]
    out_specs = [pl.BlockSpec((None, TM, tn), lambda i, j: (j, i, 0))]
    out_shape = [jax.ShapeDtypeStruct((nj, rows, tn), F32)]
    if has_extra:
        we, be = extra
        in_specs += [pl.BlockSpec((D, LANES), lambda i, j: (0, 0)),
                     pl.BlockSpec((1, LANES), lambda i, j: (0, 0))]
        args += [we, be]
        out_specs.append(pl.BlockSpec((TM, LANES), lambda i, j: (i, 0)))
        out_shape.append(jax.ShapeDtypeStruct((rows, LANES), F32))
    return pl.pallas_call(
        functools.partial(_inproj_kernel, has_extra=has_extra),
        grid=(ntiles, nj),
        in_specs=in_specs, out_specs=out_specs, out_shape=out_shape,
        scratch_shapes=[pltpu.VMEM((TM, D), BF16)],
        compiler_params=_cp("arbitrary", "arbitrary"),
        name="inproj",
    )(*args)


def _outproj_kernel(*refs, nparts, npt):
    x_ref = refs[0]
    mixp = refs[1:1 + nparts]
    mixs = refs[1 + nparts:1 + 2 * nparts]
    ws = refs[1 + 2 * nparts:1 + 3 * nparts]
    (gp, gs, nfg, shp, scp, shs, scs, wr, br, xo, ho, go, io, po, co, cnt_s) = refs[1 + 3 * nparts:]
    i = pl.program_id(0)
    isp = i < npt

    @pl.when(i == 0)
    def _():
        cnt_s[...] = jnp.zeros(cnt_s.shape, F32)

    acc = None
    for mp, ms, w in zip(mixp, mixs, ws):
        part = _bdot(jnp.where(isp, mp[...], ms[...]), w[...])
        acc = part if acc is None else acc + part
    xn = x_ref[...] + jnp.where(isp, gp[...], gs[...]) * acc
    xo[...] = xn
    y = _rms(xn, nfg[...])
    h = y * (1.0 + jnp.where(isp, scp[...], scs[...])) + jnp.where(isp, shp[...], shs[...])
    _store_tiles(ho, h)
    logits = _hdot(h, wr[...]) + br[...]
    tm = logits.shape[0]
    lane = lax.broadcasted_iota(jnp.int32, logits.shape, 1)
    l = logits
    vals, idxs = [], []
    for _ in range(TOP_K):
        m = jnp.max(l, axis=-1, keepdims=True)
        ix = jnp.min(jnp.where(l == m, lane, LANES), axis=-1, keepdims=True)
        vals.append(m)
        idxs.append(ix)
        l = jnp.where(lane == ix, -jnp.inf, l)
    es = [jnp.exp(v - vals[0]) for v in vals]
    den = es[0] + es[1] + es[2] + es[3]
    dense = jnp.zeros(logits.shape, F32)
    for k in range(TOP_K):
        dense = jnp.where(lane == idxs[k], 1.0, dense)
    before = (lax.broadcasted_iota(jnp.int32, (tm, tm), 0) > lax.broadcasted_iota(jnp.int32, (tm, tm), 1))
    rank = jnp.dot(jnp.where(before, 1.0, 0.0).astype(BF16), dense.astype(BF16),
                   preferred_element_type=F32) + cnt_s[...]
    cnt = cnt_s[...] + jnp.sum(dense, axis=0, keepdims=True)
    cnt_s[...] = cnt
    co[...] = cnt
    gates = jnp.zeros(logits.shape, F32)
    idxo = jnp.zeros(logits.shape, jnp.int32)
    poso = jnp.zeros(logits.shape, jnp.int32)
    for k in range(TOP_K):
        pk = jnp.sum(jnp.where(lane == idxs[k], rank, 0.0), axis=-1, keepdims=True)
        gates = jnp.where(lane == k, es[k] / den, gates)
        idxo = jnp.where(lane == k, idxs[k], idxo)
        poso = jnp.where(lane == k, pk.astype(jnp.int32), poso)
    go[...] = gates
    io[...] = idxo
    po[...] = poso


def _outproj(x, mixes_p, mixes_s, ws, modp, mods, nfg, wr, br, seq):
    n = x.shape[0]
    n_s = mods.shape[1]
    npt = (n - n_s) // TM
    nb = modp.shape[1]
    nparts = len(ws)
    row = lambda i: (i, 0)
    const = lambda i: (0, 0)
    in_specs = [pl.BlockSpec((TM, D), row)]
    in_specs += [pl.BlockSpec((TM, m.shape[1]), lambda i: (jnp.minimum(i, npt - 1), 0)) for m in mixes_p]
    in_specs += [pl.BlockSpec((TM, m.shape[1]), lambda i: (jnp.maximum(i - npt, 0), 0)) for m in mixes_s]
    in_specs += [pl.BlockSpec(w.shape, const) for w in ws]
    in_specs += [_modp_spec(2, TM, seq, nb), _mods_spec(2, TM, npt),
                 pl.BlockSpec((1, D), const),
                 _modp_spec(3, TM, seq, nb), _modp_spec(4, TM, seq, nb),
                 _mods_spec(3, TM, npt), _mods_spec(4, TM, npt),
                 pl.BlockSpec((D, LANES), const), pl.BlockSpec((1, LANES), const)]
    args = [x, *mixes_p, *mixes_s, *ws, modp, mods, nfg.reshape(1, D), modp, modp, mods, mods, wr, br]
    return pl.pallas_call(
        functools.partial(_outproj_kernel, nparts=nparts, npt=npt),
        grid=(n // TM,),
        in_specs=in_specs,
        out_specs=[pl.BlockSpec((TM, D), row), pl.BlockSpec((TM, D // LANES, LANES), lambda i: (i, 0, 0)),
                   pl.BlockSpec((TM, LANES), row), pl.BlockSpec((TM, LANES), row),
                   pl.BlockSpec((TM, LANES), row), pl.BlockSpec((1, LANES), const)],
        out_shape=[jax.ShapeDtypeStruct((n, D), F32), jax.ShapeDtypeStruct((n, D // LANES, LANES), F32),
                   jax.ShapeDtypeStruct((n, LANES), F32), jax.ShapeDtypeStruct((n, LANES), jnp.int32),
                   jax.ShapeDtypeStruct((n, LANES), jnp.int32), jax.ShapeDtypeStruct((1, LANES), F32)],
        scratch_shapes=[pltpu.VMEM((1, LANES), F32)],
        compiler_params=_cp("arbitrary"),
        name="outproj",
    )(*args)


def _moe_kernel(tok_ref, bexp_ref, nused_ref, h_hbm, wgu_ref, bgu_ref, wd_ref, bd_ref,
                o_ref, *rest, tm):
    nbuf = MOE_LOOKAHEAD + 1
    bufs = rest[:nbuf]
    wgu_s, wd_s, sem = rest[nbuf:]
    blk = pl.program_id(0)
    nblk = pl.num_programs(0)
    nu = nused_ref[0]

    def row_copy(b, r, buf, s):
        return pltpu.make_async_copy(h_hbm.at[pl.ds(tok_ref[b * tm + r], 1)], buf.at[pl.ds(r, 1)], sem.at[s])

    def issue_loop(b, buf, s):
        def body(r, c):
            row_copy(b, r, buf, s).start()
            return c
        lax.fori_loop(0, tm, body, 0)

    @pl.when(blk == 0)
    def _():
        for b in range(MOE_LOOKAHEAD):
            issue_loop(b, bufs[b], b)

    def step(cur, cs, nxt, ns):
        pltpu.make_async_copy(h_hbm.at[pl.ds(0, tm)], cur, sem.at[cs]).wait()

        @pl.when(blk < nu)
        def _():
            e = bexp_ref[blk]
            e_prev = bexp_ref[jnp.maximum(blk - 1, 0)]

            @pl.when((blk == 0) | (e != e_prev))
            def _():
                wgu_s[...] = wgu_ref[...].astype(BF16)
                wd_s[...] = wd_ref[...].astype(BF16)

            for r in range(tm):
                row_copy(blk + MOE_LOOKAHEAD, r, nxt, ns).start()
            xb = _load_tiles(cur).astype(BF16)
            gu = jnp.dot(xb, wgu_s[...], preferred_element_type=F32) + bgu_ref[...]
            gate = jnp.minimum(gu[:, :D], SWIGLU_LIMIT)
            up = jnp.clip(gu[:, D:], -SWIGLU_LIMIT, SWIGLU_LIMIT)
            act = (up + 1.0) * gate * _sigmoid(SWIGLU_ALPHA * gate)
            o_ref[...] = jnp.dot(act.astype(BF16), wd_s[...], preferred_element_type=F32) + bd_ref[...]

        @pl.when(blk >= nu)
        def _():
            @pl.when(blk + MOE_LOOKAHEAD < nblk)
            def _():
                issue_loop(blk + MOE_LOOKAHEAD, nxt, ns)
            o_ref[...] = jnp.zeros(o_ref.shape, F32)

    for k in range(nbuf):
        ahead = (k + MOE_LOOKAHEAD) % nbuf
        pl.when(blk % nbuf == k)(functools.partial(step, bufs[k], k, bufs[ahead], ahead))


def _moe_experts(h, slot_tok, block_exp, nused, layer, w_gu, b_gu, w_down, b_down):
    tm = MOE_TM
    nb = block_exp.shape[0]
    nl = w_gu.shape[0]
    grid_spec = pltpu.PrefetchScalarGridSpec(
        num_scalar_prefetch=3,
        grid=(nb,),
        in_specs=[pl.BlockSpec(memory_space=pl.ANY),
                  pl.BlockSpec((None, None, D, 2 * D), lambda b, t, e, u: (layer, e[b], 0, 0)),
                  pl.BlockSpec((None, None, 1, 2 * D), lambda b, t, e, u: (layer, e[b], 0, 0)),
                  pl.BlockSpec((None, None, D, D), lambda b, t, e, u: (layer, e[b], 0, 0)),
                  pl.BlockSpec((None, None, 1, D), lambda b, t, e, u: (layer, e[b], 0, 0))],
        out_specs=pl.BlockSpec((tm, D), lambda b, t, e, u: (b, 0)),
        scratch_shapes=[pltpu.VMEM((tm, D // LANES, LANES), F32) for _ in range(MOE_LOOKAHEAD + 1)]
        + [pltpu.VMEM((D, 2 * D), BF16),
           pltpu.VMEM((D, D), BF16),
           pltpu.SemaphoreType.DMA((MOE_LOOKAHEAD + 1,))])
    return pl.pallas_call(
        functools.partial(_moe_kernel, tm=tm),
        grid_spec=grid_spec,
        out_shape=jax.ShapeDtypeStruct((nb * tm, D), F32),
        compiler_params=_cp("arbitrary"),
        name="moe_experts",
    )(slot_tok, block_exp, nused, h, w_gu,
      b_gu.reshape(nl, N_EXPERTS, 1, 2 * D), w_down, b_down.reshape(nl, N_EXPERTS, 1, D))


def _combine_kernel(sof_ref, y_hbm, x_ref, gt_ref, gp, gs, fg_ref, o_ref, y0, y1, sem, *, tm, npt, final):
    i = pl.program_id(0)
    nt = pl.num_programs(0)

    def issue(t, buf, s, unroll):
        base = t * (tm * TOP_K)

        def body(r, c):
            for k in range(TOP_K):
                slot = sof_ref[base + r * TOP_K + k]
                pltpu.make_async_copy(y_hbm.at[pl.ds(slot, 1)], buf.at[k, pl.ds(r, 1)], sem.at[s]).start()
            return c

        lax.fori_loop(0, tm, body, 0, unroll=unroll)

    @pl.when(i == 0)
    def _():
        issue(0, y0, 0, 4)

    def step(cur, cs, nxt, ns):
        @pl.when(i + 1 < nt)
        def _():
            issue(i + 1, nxt, ns, CMB_UNROLL)

        for k in range(TOP_K):
            pltpu.make_async_copy(y_hbm.at[pl.ds(0, tm)], cur.at[k], sem.at[cs]).wait()
        gt = gt_ref[...]
        y = ((cur[0] * gt[:, 0:1] + cur[1] * gt[:, 1:2]) + (cur[2] * gt[:, 2:3] + cur[3] * gt[:, 3:4]))
        xn = x_ref[...] + jnp.where(i < npt, gp[...], gs[...]) * y
        if final:
            xn = _rms(xn, fg_ref[...])
        o_ref[...] = xn

    pl.when(i % 2 == 0)(functools.partial(step, y0, 0, y1, 1))
    pl.when(i % 2 == 1)(functools.partial(step, y1, 1, y0, 0))


def _combine(x, y_slots, slot_of, gates, modp, mods, final_g, seq, final):
    tm = CMB_TM
    n = x.shape[0]
    n_s = mods.shape[1]
    npt = (n - n_s) // tm
    nb = modp.shape[1]
    grid_spec = pltpu.PrefetchScalarGridSpec(
        num_scalar_prefetch=1,
        grid=(n // tm,),
        in_specs=[pl.BlockSpec(memory_space=pl.ANY),
                  pl.BlockSpec((tm, D), lambda i, s: (i, 0)),
                  pl.BlockSpec((tm, LANES), lambda i, s: (i, 0)),
                  _modp_spec(5, tm, seq, nb), _mods_spec(5, tm, npt),
                  pl.BlockSpec((1, D), lambda i, s: (0, 0))],
        out_specs=pl.BlockSpec((tm, D), lambda i, s: (i, 0)),
        scratch_shapes=[pltpu.VMEM((TOP_K, tm, D), F32), pltpu.VMEM((TOP_K, tm, D), F32),
                        pltpu.SemaphoreType.DMA((2,))])
    return pl.pallas_call(
        functools.partial(_combine_kernel, tm=tm, npt=npt, final=final),
        grid_spec=grid_spec,
        out_shape=jax.ShapeDtypeStruct((n, D), F32),
        compiler_params=_cp("arbitrary"),
        name="moe_combine",
    )(slot_of, y_slots, x, gates, modp, mods, final_g.reshape(1, D))


def _moe_meta(idx4, pos4, counts, tm):
    n = idx4.shape[0]
    nk = n * TOP_K
    nb = -(-(nk + N_EXPERTS * (tm - 1)) // tm) + MOE_LOOKAHEAD
    padded = (counts + tm - 1) // tm * tm
    pad_end = jnp.cumsum(padded)
    pad_start = pad_end - padded
    flat = (pad_start[idx4] + pos4).astype(jnp.int32).reshape(nk)
    tok = jnp.repeat(jnp.arange(n, dtype=jnp.int32), TOP_K)
    slot_tok = jnp.zeros((nb * tm,), jnp.int32).at[flat].set(tok)
    starts = jnp.arange(nb, dtype=jnp.int32) * tm
    block_exp = jnp.minimum(jnp.sum((pad_end[None, :] <= starts[:, None]).astype(jnp.int32), axis=1),
                            N_EXPERTS - 1).astype(jnp.int32)
    nused = (pad_end[-1] // tm).astype(jnp.int32).reshape(1)
    return slot_tok, block_exp, nused, flat


def _seg_consts():
    r = np.arange(256)
    mseg = (r[:, None] // A_HD == r[None, :] // A_HD).astype(np.float32)
    c = np.arange(A_W)
    eyet = (np.arange(A_HD)[:, None] == (c[None, :] % A_HD)).astype(np.float32)
    return jnp.asarray(mseg, BF16), jnp.asarray(eyet, F32)


def _split(x):
    hi = x.astype(BF16)
    lo = (x - hi.astype(F32)).astype(BF16)
    return hi, lo


def _segsum(x, mseg):
    hi, lo = _split(x)
    halves = []
    for hf in range(2):
        sl = slice(hf * 256, (hf + 1) * 256)
        halves.append(jnp.dot(hi[:, sl], mseg, preferred_element_type=F32)
                      + jnp.dot(lo[:, sl], mseg, preferred_element_type=F32))
    return jnp.concatenate(halves, axis=1)


def _rwkv_kernel(*refs, nseq, tc, has_vres, ngrp):
    it = iter(refs)
    p_ref, st_ref, s0_ref = next(it), next(it), next(it)
    vf_ref = next(it) if has_vres else None
    mu, w0, w2, a0, a2, g2, kkp, kap, rk, lng, lnb = (next(it) for _ in range(11))
    if has_vres:
        v0, v1, v2 = next(it), next(it), next(it)
    mseg_ref, eyet_ref = next(it), next(it)
    o_ref, sout_ref, shout_ref = next(it), next(it), next(it)
    vfo_ref = None if has_vres else next(it)
    w_s, k_s, v_s, nkk_s, b_s, r_s, out_s, st_s, prev_s = (next(it) for _ in range(9))

    c = pl.program_id(1)
    nch = pl.num_programs(1)
    rows = nseq * tc
    mseg = mseg_ref[...]
    mseg2 = jnp.concatenate([mseg, mseg], axis=0)
    eyet = eyet_ref[...]

    @pl.when(c == 0)
    def _():
        prev_s[...] = st_ref[...]
        for s in range(nseq):
            for h in range(A_HEADS):
                st_s[s, :, h * A_HD:(h + 1) * A_HD] = s0_ref[s, h]

    p = p_ref[:, :, :RWKV_PROJ].reshape(rows, RWKV_PROJ)
    rolled = pltpu.roll(p, 1, 0)
    ridx = lax.broadcasted_iota(jnp.int32, (rows, 1), 0)
    sidx = lax.broadcasted_iota(jnp.int32, (rows, nseq), 1)
    onehot = (lax.broadcasted_iota(jnp.int32, (rows, nseq), 0) == sidx * tc).astype(F32)
    prev = jnp.where(ridx % tc == 0, _hdot(onehot, prev_s[...]), rolled)
    z = p + (prev - p) * mu[...]
    r = z[:, 0:512]
    k = z[:, 512:1024]
    v = z[:, 1024:1536]
    wi = z[:, 1536:1600]
    ai = z[:, 1600:1664]
    gi = z[:, 1664:1792]
    w_log = -_softplus(-(w0[...] + _hdot(jnp.tanh(wi), w2[...]))) - 0.5
    w = jnp.exp(-jnp.exp(w_log))
    a = _sigmoid(a0[...] + _hdot(ai, a2[...]))
    g = _bdot(_sigmoid(gi), g2[...])
    if has_vres:
        vf = vf_ref[...].reshape(rows, A_W)
        v = v + (vf - v) * _sigmoid(v0[...] + _bdot(_bdot(v, v1[...]), v2[...]))
    else:
        vfo_ref[...] = v.reshape(nseq, tc, A_W)
    kk = k * kkp[...]
    kk = kk * lax.rsqrt(jnp.maximum(_segsum(kk * kk, mseg), 1e-24))
    k = k * (1.0 + (a - 1.0) * kap[...])
    w_s[...] = w
    k_s[...] = k
    v_s[...] = v
    nkk_s[...] = -kk
    b_s[...] = kk * a
    r_s[...] = r
    bonus = _segsum(r * k * rk[...], mseg) * v

    def halves(x):
        return [x[:, 0:256], x[:, 256:512]]

    def token(t, carry):
        groups = [range(g0, g0 + nseq // ngrp) for g0 in range(0, nseq, nseq // ngrp)]
        ress = []
        for grp in groups:
            lhs, lhv = [], []
            for s in grp:
                row = s * tc + t
                x1h, x1l = _split(st_s[s] * nkk_s[pl.ds(row, 1), :])
                lhs += [jnp.concatenate([a, b], axis=1) for a, b in zip(halves(x1h), halves(x1l))]
                lhv += halves((eyet * v_s[pl.ds(row, 1), :]).astype(BF16))
            ress.append((jnp.dot(jnp.concatenate(lhs, axis=0), mseg2, preferred_element_type=F32),
                         jnp.dot(jnp.concatenate(lhv, axis=0), mseg, preferred_element_type=F32)))
        ress2 = []
        for grp, (res, resv) in zip(groups, ress):
            lhs2 = []
            for n, s in enumerate(grp):
                row = s * tc + t
                o = n * 2 * A_HD
                sa = jnp.concatenate([res[o:o + A_HD], res[o + A_HD:o + 2 * A_HD]], axis=1)
                v2 = jnp.concatenate([resv[o:o + A_HD], resv[o + A_HD:o + 2 * A_HD]], axis=1)
                st = (st_s[s] * w_s[pl.ds(row, 1), :] + sa * b_s[pl.ds(row, 1), :]
                      + v2 * k_s[pl.ds(row, 1), :])
                st_s[s] = st
                lhs2 += halves((st * r_s[pl.ds(row, 1), :]).astype(BF16))
            ress2.append(jnp.dot(jnp.concatenate(lhs2, axis=0), mseg, preferred_element_type=F32))
        for grp, res2 in zip(groups, ress2):
            for n, s in enumerate(grp):
                row = s * tc + t
                o = n * 2 * A_HD
                o2 = jnp.concatenate([res2[o:o + A_HD], res2[o + A_HD:o + 2 * A_HD]], axis=1)
                out_s[pl.ds(row, 1), :] = jnp.sum(o2 * eyet, axis=0, keepdims=True)
        return carry

    lax.fori_loop(0, tc, token, 0)

    out = out_s[...]
    xc = out - _segsum(out, mseg) * (1.0 / A_HD)
    var = _segsum(xc * xc, mseg) * (1.0 / A_HD)
    y = xc * lax.rsqrt(var + LNX_EPS) * lng[...] + lnb[...]
    o_ref[...] = ((y + bonus) * g).reshape(nseq, tc, A_W)

    for s in range(nseq):
        prev_s[s:s + 1, :] = p[s * tc + tc - 1:s * tc + tc, :]

    @pl.when(c == nch - 1)
    def _():
        shout_ref[...] = prev_s[...]
        for s in range(nseq):
            for h in range(A_HEADS):
                sout_ref[s, h] = st_s[s, :, h * A_HD:(h + 1) * A_HD]


def _rwkv(proj, nb, seq, nseq, tc, st_shift, s0, vfirst, lp, vres):
    nch = seq // tc
    rows = nseq * tc
    has_vres = vres is not None
    blk3 = lambda b, c: (b, c, 0)
    vec = lambda a: a.reshape(1, -1)
    c2 = lambda b, c: (0, 0)
    mseg, eyet = _seg_consts()
    in_specs = [pl.BlockSpec((None, nseq, tc, 2048), lambda b, c: (0, b, c, 0)),
                pl.BlockSpec((None, nseq, RWKV_PROJ), lambda b, c: (b, 0, 0)),
                pl.BlockSpec((nseq, A_HEADS, A_HD, A_HD), lambda b, c: (b, 0, 0, 0))]
    args = [proj.reshape(proj.shape[0], nb, seq, 2048), st_shift.reshape(nb // nseq, nseq, RWKV_PROJ), s0]
    if has_vres:
        in_specs.append(pl.BlockSpec((nseq, tc, A_W), blk3))
        args.append(vfirst)
    small = [vec(lp['mu']), vec(lp['w0']), lp['w2'], vec(lp['a0']), lp['a2'], lp['g2'].astype(BF16),
             vec(lp['k_k']), vec(lp['k_a']), vec(lp['r_k']), vec(lp['ln_g']), vec(lp['ln_b'])]
    if has_vres:
        small += [vec(vres[0]), vres[1].astype(BF16), vres[2].astype(BF16)]
    small += [mseg, eyet]
    in_specs += [pl.BlockSpec(a.shape, c2) for a in small]
    args += small
    out_specs = [pl.BlockSpec((nseq, tc, A_W), blk3),
                 pl.BlockSpec((nseq, A_HEADS, A_HD, A_HD), lambda b, c: (b, 0, 0, 0)),
                 pl.BlockSpec((None, nseq, RWKV_PROJ), lambda b, c: (b, 0, 0))]
    out_shape = [jax.ShapeDtypeStruct((nb, seq, A_W), F32),
                 jax.ShapeDtypeStruct((nb, A_HEADS, A_HD, A_HD), F32),
                 jax.ShapeDtypeStruct((nb // nseq, nseq, RWKV_PROJ), F32)]
    if not has_vres:
        out_specs.append(pl.BlockSpec((nseq, tc, A_W), blk3))
        out_shape.append(jax.ShapeDtypeStruct((nb, seq, A_W), F32))
    scratch = [pltpu.VMEM((rows, A_W), F32) for _ in range(7)]
    scratch += [pltpu.VMEM((nseq, A_HD, A_W), F32), pltpu.VMEM((nseq, RWKV_PROJ), F32)]
    outs = pl.pallas_call(
        functools.partial(_rwkv_kernel, nseq=nseq, tc=tc, has_vres=has_vres, ngrp=RW_GROUPS),
        grid=(nb // nseq, nch),
        in_specs=in_specs, out_specs=out_specs, out_shape=out_shape,
        scratch_shapes=scratch,
        compiler_params=_cp("arbitrary", "arbitrary"),
        name="rwkv",
    )(*args)
    o, s_out, sh_out = outs[0].reshape(nb * seq, A_W), outs[1], outs[2].reshape(nb, RWKV_PROJ)
    vf_out = vfirst if has_vres else outs[3]
    return o, s_out, sh_out, vf_out


def _ret_kernel(pr_ref, s0_ref, cos_ref, sin_ref, gn_ref, o_ref, sout_ref, st_s, *, nseq, L):
    c = pl.program_id(1)
    nch = pl.num_programs(1)

    @pl.when(c == 0)
    def _():
        st_s[...] = s0_ref[...]

    cos = cos_ref[...]
    sin = sin_ref[...]
    ii = lax.broadcasted_iota(jnp.int32, (L, L), 0)
    jj = lax.broadcasted_iota(jnp.int32, (L, L), 1)
    dif = (ii - jj).astype(F32)
    ri = lax.broadcasted_iota(jnp.int32, (L, 1), 0).astype(F32)

    def rot(x):
        return x * cos + pltpu.roll(x, B_HD // 2, 1) * sin

    for h in range(B_HEADS):
        lg = math.log1p(-2.0 ** (-5.0 - h))
        intra = jnp.where(dif >= 0, jnp.exp(jnp.maximum(dif, 0.0) * lg), 0.0)
        q_dec = jnp.exp((ri + 1.0) * lg)
        k_dec = jnp.exp((L - 1.0 - ri) * lg)
        c_dec = math.exp(L * lg)
        gn = gn_ref[:, h * B_HD:(h + 1) * B_HD]
        for s in range(nseq):
            rs = slice(s * L, (s + 1) * L)
            q = rot(pr_ref[rs, h * B_HD:(h + 1) * B_HD])
            k = rot(pr_ref[rs, 512 + h * B_HD:512 + (h + 1) * B_HD]) * (B_HD ** -0.5)
            v = pr_ref[rs, 1024 + h * B_HD:1024 + (h + 1) * B_HD]
            g = pr_ref[rs, 1536 + h * B_HD:1536 + (h + 1) * B_HD]
            st = st_s[s, h]
            vb = v.astype(BF16)
            att = _nt_dot(q.astype(BF16), k.astype(BF16)) * intra
            o = (jnp.dot(att.astype(BF16), vb, preferred_element_type=F32)
                 + _bdot(q * q_dec, st))
            st_s[s, h] = st * c_dec + lax.dot_general(
                (k * k_dec).astype(BF16), vb, (((0,), (0,)), ((), ())), preferred_element_type=F32)
            oc = o - jnp.mean(o, axis=-1, keepdims=True)
            y = oc * lax.rsqrt(jnp.mean(oc * oc, axis=-1, keepdims=True) + GN_EPS) * gn
            o_ref[rs, h * B_HD:(h + 1) * B_HD] = g * _sigmoid(g) * y

    @pl.when(c == nch - 1)
    def _():
        sout_ref[...] = st_s[...]


def _rot_tables(pos0, t):
    half = B_HD // 2
    inv = 1.0 / (10000.0 ** jnp.linspace(0.0, 1.0, half, dtype=F32))
    ang = (pos0 + jnp.arange(t, dtype=F32))[:, None] * inv[None, :]
    cos, sin = jnp.cos(ang), jnp.sin(ang)
    return jnp.concatenate([cos, cos], axis=1), jnp.concatenate([-sin, sin], axis=1)


def _retention(proj, nb, seq, nseq, L, pos0, s0, gn_g):
    n = proj.shape[1]
    nch = seq // L
    rows = nseq * L
    cos, sin = _rot_tables(float(pos0), seq)
    rowmap = lambda b, c: (b * nch + c, 0)
    return pl.pallas_call(
        functools.partial(_ret_kernel, nseq=nseq, L=L),
        grid=(nb // nseq, nch),
        in_specs=[pl.BlockSpec((None, rows, 2048), lambda b, c: (1, b * nch + c, 0)),
                  pl.BlockSpec((nseq, B_HEADS, B_HD, B_HD), lambda b, c: (b, 0, 0, 0)),
                  pl.BlockSpec((L, B_HD), lambda b, c: (c, 0)),
                  pl.BlockSpec((L, B_HD), lambda b, c: (c, 0)),
                  pl.BlockSpec((1, 512), lambda b, c: (0, 0))],
        out_specs=[pl.BlockSpec((rows, 512), rowmap),
                   pl.BlockSpec((nseq, B_HEADS, B_HD, B_HD), lambda b, c: (b, 0, 0, 0))],
        out_shape=[jax.ShapeDtypeStruct((n, 512), F32),
                   jax.ShapeDtypeStruct((nb, B_HEADS, B_HD, B_HD), F32)],
        scratch_shapes=[pltpu.VMEM((nseq, B_HEADS, B_HD, B_HD), F32)],
        compiler_params=_cp("arbitrary", "arbitrary"),
        name="retention",
    )(proj, s0, cos, sin, gn_g.reshape(1, 512))


def _tri(n):
    return (lax.broadcasted_iota(jnp.int32, (n, n), 0) >= lax.broadcasted_iota(jnp.int32, (n, n), 1)).astype(F32)


def _cumsum_kernel(lf_ref, o_ref):
    t = lf_ref.shape[0]
    tri = _tri(LANES)
    carry = jnp.zeros((1, LANES), F32)
    for b in range(t // LANES):
        rs = slice(b * LANES, (b + 1) * LANES)
        cb = _hdot(tri, lf_ref[rs, :]) + carry
        o_ref[rs, :] = cb
        carry = cb[LANES - 1:LANES, :]


def _cumsum_prompt(logf, nb, seq):
    return pl.pallas_call(
        _cumsum_kernel,
        grid=(nb,),
        in_specs=[pl.BlockSpec((seq, LANES), lambda b: (b, 0))],
        out_specs=pl.BlockSpec((seq, LANES), lambda b: (b, 0)),
        out_shape=jax.ShapeDtypeStruct((nb * seq, LANES), F32),
        compiler_params=_cp("arbitrary"),
        name="fox_cumsum",
    )(logf)


def _foxp_kernel(q_ref, k_ref, v_ref, cq_ref, ck_ref, o_ref, *, tq):
    hp = pl.program_id(1)
    qi = pl.program_id(2)
    lane = lax.broadcasted_iota(jnp.int32, (tq, LANES), 1)
    qs, cqs = [], []
    for hh in range(2):
        hs = slice(hh * C_HD, (hh + 1) * C_HD)
        qs.append((q_ref[:, hs] * (C_HD ** -0.5)).astype(BF16))
        cqs.append(jnp.sum(jnp.where(lane == hp * 2 + hh, cq_ref[...], 0.0), axis=-1, keepdims=True))

    def block(off, diag, carry):
        new = []
        for hh in range(2):
            m, l, acc = carry[3 * hh:3 * hh + 3]
            hs = slice(hh * C_HD, (hh + 1) * C_HD)
            k = k_ref[pl.ds(off, tq), hs].astype(BF16)
            v = v_ref[pl.ds(off, tq), hs].astype(BF16)
            s = _nt_dot(qs[hh], k) + (cqs[hh] - ck_ref[pl.ds(hp * 2 + hh, 1), pl.ds(off, tq)])
            if diag:
                s = jnp.where(lax.broadcasted_iota(jnp.int32, (tq, tq), 0)
                              >= lax.broadcasted_iota(jnp.int32, (tq, tq), 1), s, NEG)
            m_new = jnp.maximum(m, jnp.max(s, axis=-1, keepdims=True))
            alpha = jnp.exp(m - m_new)
            p = jnp.exp(s - m_new)
            new += [m_new, alpha * l + jnp.sum(p, axis=-1, keepdims=True),
                    alpha * acc + jnp.dot(p.astype(BF16), v, preferred_element_type=F32)]
        return tuple(new)

    init = (jnp.full((tq, 1), NEG, F32), jnp.zeros((tq, 1), F32), jnp.zeros((tq, C_HD), F32)) * 2
    carry = lax.fori_loop(0, qi, lambda kj, c: block(pl.multiple_of(kj * tq, tq), False, c), init)
    carry = block(pl.multiple_of(qi * tq, tq), True, carry)
    o_ref[...] = jnp.concatenate([carry[2] / carry[1], carry[5] / carry[4]], axis=1)


def _fox_prompt(qkv, c, ct, nb, seq):
    n = qkv.shape[1]
    tq = FOX_TQ
    nq = seq // tq
    return pl.pallas_call(
        functools.partial(_foxp_kernel, tq=tq),
        grid=(nb, C_HEADS // 2, nq),
        in_specs=[pl.BlockSpec((None, tq, LANES), lambda b, hp, qi: (0, b * nq + qi, hp)),
                  pl.BlockSpec((None, seq, LANES), lambda b, hp, qi: (1, b, hp)),
                  pl.BlockSpec((None, seq, LANES), lambda b, hp, qi: (2, b, hp)),
                  pl.BlockSpec((tq, LANES), lambda b, hp, qi: (b * nq + qi, 0)),
                  pl.BlockSpec((None, C_HEADS, seq), lambda b, hp, qi: (b, 0, 0))],
        out_specs=pl.BlockSpec((tq, LANES), lambda b, hp, qi: (b * nq + qi, hp)),
        out_shape=jax.ShapeDtypeStruct((n, D), F32),
        compiler_params=_cp("arbitrary", "arbitrary", "arbitrary"),
        name="fox_prompt",
    )(qkv, qkv, qkv, c, ct)


def _foxs_kernel(pt_ref, q_ref, kn_ref, vn_ref, lfn_ref, *rest, t_new, npp):
    kts, vts, clfs = rest[0:npp], rest[npp:2 * npp], rest[2 * npp:3 * npp]
    o_ref, qbd, cnq_s, m_s, l_s, acc_s, carry = rest[3 * npp:]
    g = pl.program_id(1)
    nrow = C_HEADS * t_new
    page = clfs[0].shape[1]
    rowh = lax.broadcasted_iota(jnp.int32, (nrow, D), 0) // t_new
    bmask = (lax.broadcasted_iota(jnp.int32, (nrow, D), 1) // C_HD) == rowh

    def per_row(x):
        return jnp.concatenate([jnp.broadcast_to(x[h:h + 1, :], (t_new, x.shape[1])) for h in range(C_HEADS)],
                               axis=0)

    def update(s, pv):
        m = m_s[...]
        m_new = jnp.maximum(m, jnp.max(s, axis=-1, keepdims=True))
        alpha = jnp.exp(m - m_new)
        p = jnp.exp(s - m_new)
        l_s[...] = alpha * l_s[...] + jnp.sum(p, axis=-1, keepdims=True)
        acc_s[...] = alpha * acc_s[...] + pv(p)
        m_s[...] = m_new

    @pl.when(g == 0)
    def _():
        q = q_ref[...] * (C_HD ** -0.5)
        qbd[...] = jnp.where(bmask, jnp.concatenate([q] * C_HEADS, axis=0), 0.0).astype(BF16)
        m_s[...] = jnp.full(m_s.shape, NEG, F32)
        l_s[...] = jnp.zeros(l_s.shape, F32)
        acc_s[...] = jnp.zeros(acc_s.shape, F32)
        carry[...] = jnp.zeros(carry.shape, F32)
        tle = (lax.broadcasted_iota(jnp.int32, (t_new, t_new), 0)
               <= lax.broadcasted_iota(jnp.int32, (t_new, t_new), 1)).astype(F32)
        cnt = lax.dot_general(lfn_ref[:, :C_HEADS], tle, (((0,), (0,)), ((), ())),
                              precision=HI, preferred_element_type=F32)
        gk = per_row(cnt)
        colt = lax.broadcasted_iota(jnp.int32, (nrow, t_new), 1)
        qrow = lax.broadcasted_iota(jnp.int32, (nrow, t_new), 0) % t_new
        cnq = jnp.sum(jnp.where(colt == qrow, gk, 0.0), axis=-1, keepdims=True)
        cnq_s[...] = cnq
        s2 = _nt_dot(qbd[...], kn_ref[...].astype(BF16)) + cnq - gk
        update(jnp.where(colt <= qrow, s2, NEG),
               lambda p: jnp.dot(p, vn_ref[...], preferred_element_type=F32))

    later = (lax.broadcasted_iota(jnp.int32, (page, page), 0)
             > lax.broadcasted_iota(jnp.int32, (page, page), 1)).astype(F32)
    lfs = [r[...] for r in clfs]
    suf = _hdot(jnp.concatenate(lfs, axis=0), later)
    run = carry[...]
    ds = []
    for i in range(npp):
        si = suf[i * C_HEADS:(i + 1) * C_HEADS]
        ds.append(si + run)
        run = run + si[:, 0:1] + lfs[i][:, 0:1]
    carry[...] = run
    qb = qbd[...]
    s = jnp.concatenate([jnp.dot(qb, kt[...].astype(BF16), preferred_element_type=F32) for kt in kts], axis=1)
    s = s + (per_row(jnp.concatenate(ds, axis=1)) + cnq_s[...])

    def pv(p):
        pb = p.astype(BF16)
        out = _nt_dot(pb[:, 0:page], vts[0][...].astype(BF16))
        for i in range(1, npp):
            out = out + _nt_dot(pb[:, i * page:(i + 1) * page], vts[i][...].astype(BF16))
        return out

    update(s, pv)

    @pl.when(g == pl.num_programs(1) - 1)
    def _():
        o = jnp.where(bmask, acc_s[...] / l_s[...], 0.0)
        o_ref[...] = jnp.sum(o.reshape(C_HEADS, t_new, D), axis=0)


def _fox_sample(qkv, logf, nb, t_new, page_table, layer, kt, vt, clft):
    n = qkv.shape[1]
    npg = page_table.shape[1]
    page = kt.shape[3]
    nrow = C_HEADS * t_new
    npp = FOX_PAGES

    def pidx(i):
        return lambda b, g, pt: (layer, pt[b * npg + (npg - 1 - (g * npp + i))], 0, 0)

    in_specs = [pl.BlockSpec((None, t_new, D), lambda b, g, pt: (0, b, 0)),
                pl.BlockSpec((None, t_new, D), lambda b, g, pt: (1, b, 0)),
                pl.BlockSpec((None, t_new, D), lambda b, g, pt: (2, b, 0)),
                pl.BlockSpec((t_new, LANES), lambda b, g, pt: (b, 0))]
    in_specs += [pl.BlockSpec((None, None, D, page), pidx(i)) for i in range(npp)]
    in_specs += [pl.BlockSpec((None, None, D, page), pidx(i)) for i in range(npp)]
    in_specs += [pl.BlockSpec((None, None, C_HEADS, page), pidx(i)) for i in range(npp)]
    grid_spec = pltpu.PrefetchScalarGridSpec(
        num_scalar_prefetch=1,
        grid=(nb, npg // npp),
        in_specs=in_specs,
        out_specs=pl.BlockSpec((t_new, D), lambda b, g, pt: (b, 0)),
        scratch_shapes=[pltpu.VMEM((nrow, D), BF16), pltpu.VMEM((nrow, 1), F32), pltpu.VMEM((nrow, 1), F32),
                        pltpu.VMEM((nrow, 1), F32), pltpu.VMEM((nrow, D), F32), pltpu.VMEM((C_HEADS, 1), F32)])
    return pl.pallas_call(
        functools.partial(_foxs_kernel, t_new=t_new, npp=npp),
        grid_spec=grid_spec,
        out_shape=jax.ShapeDtypeStruct((n, D), F32),
        compiler_params=_cp("arbitrary", "arbitrary"),
        name="fox_sample",
    )(page_table.reshape(-1), qkv, qkv, qkv, logf, *([kt] * npp), *([vt] * npp), *([clft] * npp))


def kernel(x_prompt, x_sample, c_prompt, c_sample, state_rwkv, state_rwkv_shift, state_retention, cache_fox_k, cache_fox_v, cache_fox_logf, page_table, w_ada, b_ada, norm_mix_g, norm_ffn_g, final_g, w_in_even, w_out_even, rw_mu, rw_w0, rw_w2, rw_a0, rw_a2, rw_g2, rw_kk, rw_ka, rw_rk, rw_ln_g, rw_ln_b, rw_v0, rw_v1, rw_v2, ret_gn_g, w_in_odd, b_forget, w_out_odd, w_router, b_router, w_gu, b_gu, w_down, b_down):
    bp, seq, _ = x_prompt.shape
    bs, tn_, _ = x_sample.shape
    n_p, n_s = bp * seq, bs * tn_
    npt, nst = n_p // TM, n_s // TM
    depth = w_ada.shape[0]
    n_phys, page = cache_fox_k.shape[1], cache_fox_k.shape[2]
    npg = page_table.shape[1]
    past_len = npg * page
    n_odd = cache_fox_k.shape[0]
    kt_all = jnp.transpose(cache_fox_k, (0, 1, 3, 4, 2)).reshape(n_odd, n_phys, D, page)
    vt_all = jnp.transpose(cache_fox_v, (0, 1, 3, 4, 2)).reshape(n_odd, n_phys, D, page)
    clft_all = jnp.transpose(cache_fox_logf, (0, 1, 3, 2))

    x = jnp.concatenate([x_prompt.reshape(n_p, D), x_sample.reshape(n_s, D)], axis=0)
    ada = _ada(jnp.concatenate([c_prompt, c_sample], axis=0), w_ada, b_ada)

    zero_shift = jnp.zeros((bp, RWKV_PROJ), F32)
    zero_rw = jnp.zeros((bp, A_HEADS, A_HD, A_HD), F32)
    zero_ret = jnp.zeros((bp, B_HEADS, B_HD, B_HD), F32)
    new = {k: [] for k in ('rw_S_p', 'rw_S_s', 'sh_p', 'sh_s', 'ret_p', 'ret_s',
                           'k_p', 'k_s', 'v_p', 'v_s', 'lf_p', 'lf_s')}
    vf_p = vf_s = None
    for i in range(depth):
        modp = ada[i, :, :bp].reshape(6, bp, 1, D)
        mods = jnp.repeat(ada[i, :, bp:], tn_, axis=1)
        both = lambda w, tn, extra=None: (
            _inproj(x, norm_mix_g[i], modp, False, 0, npt, seq, w, tn, extra),
            _inproj(x, norm_mix_g[i], mods, True, npt, nst, seq, w, tn, extra))
        if i % 2 == 0:
            e = i // 2
            w_in = w_in_even[e]
            w_pad = jnp.concatenate([w_in[:, :RWKV_PROJ], jnp.zeros((D, 2048 - RWKV_PROJ), F32),
                                     w_in[:, RWKV_PROJ:]], axis=1).astype(BF16)
            (proj_p,), (proj_s,) = both(w_pad, 2048)
            lp = {'mu': rw_mu[e], 'w0': rw_w0[e], 'w2': rw_w2[e], 'a0': rw_a0[e], 'a2': rw_a2[e],
                  'g2': rw_g2[e], 'k_k': rw_kk[e], 'k_a': rw_ka[e], 'r_k': rw_rk[e],
                  'ln_g': rw_ln_g[e], 'ln_b': rw_ln_b[e]}
            vres = None if e == 0 else (rw_v0[e - 1], rw_v1[e - 1], rw_v2[e - 1])
            oa_p, s_p, sh_p, vf1_p = _rwkv(proj_p, bp, seq, RW_NSEQ, RW_TC, zero_shift, zero_rw, vf_p, lp, vres)
            oa_s, s_s, sh_s, vf1_s = _rwkv(proj_s, bs, tn_, RW_NSEQ, tn_, state_rwkv_shift[e], state_rwkv[e],
                                           vf_s, lp, vres)
            if e == 0:
                vf_p, vf_s = vf1_p, vf1_s
            ob_p, r_p = _retention(proj_p, bp, seq, 1, RET_L, 0, zero_ret, ret_gn_g[e])
            ob_s, r_s = _retention(proj_s, bs, tn_, 8, tn_, past_len, state_retention[e], ret_gn_g[e])
            new['rw_S_p'].append(s_p)
            new['rw_S_s'].append(s_s)
            new['sh_p'].append(sh_p)
            new['sh_s'].append(sh_s)
            new['ret_p'].append(r_p)
            new['ret_s'].append(r_s)
            w_out = w_out_even[e].astype(BF16)
            mixes_p, mixes_s, ws = [oa_p, ob_p], [oa_s, ob_s], [w_out[:A_W], w_out[A_W:]]
        else:
            j = i // 2
            w_in = w_in_odd[j]
            we = jnp.pad(w_in[:, 3 * D:], ((0, 0), (0, LANES - C_HEADS))).astype(BF16)
            be = jnp.pad(b_forget[j], (0, LANES - C_HEADS)).reshape(1, LANES)
            (qkv_p, lf_p), (qkv_s, lf_s) = both(w_in[:, :3 * D].astype(BF16), D, (we, be))
            c = _cumsum_prompt(lf_p, bp, seq)
            ct = jnp.transpose(c[:, :C_HEADS].reshape(bp, seq, C_HEADS), (0, 2, 1))
            o_p = _fox_prompt(qkv_p, c, ct, bp, seq)
            o_s = _fox_sample(qkv_s, lf_s, bs, tn_, page_table, j, kt_all, vt_all, clft_all)
            new['k_p'].append(qkv_p[1].reshape(bp, seq, C_HEADS, C_HD))
            new['k_s'].append(qkv_s[1].reshape(bs, tn_, C_HEADS, C_HD))
            new['v_p'].append(qkv_p[2].reshape(bp, seq, C_HEADS, C_HD))
            new['v_s'].append(qkv_s[2].reshape(bs, tn_, C_HEADS, C_HD))
            new['lf_p'].append(lf_p[:, :C_HEADS].reshape(bp, seq, C_HEADS))
            new['lf_s'].append(lf_s[:, :C_HEADS].reshape(bs, tn_, C_HEADS))
            mixes_p, mixes_s, ws = [o_p], [o_s], [w_out_odd[j].astype(BF16)]
        wr = jnp.pad(w_router[i], ((0, 0), (0, LANES - N_EXPERTS)))
        br = jnp.concatenate([b_router[i], jnp.full((LANES - N_EXPERTS,), NEG, F32)]).reshape(1, LANES)
        x, h, gates, idx, pos, cnt = _outproj(x, mixes_p, mixes_s, ws, modp, mods, norm_ffn_g[i], wr, br, seq)
        counts = cnt[0, :N_EXPERTS].astype(jnp.int32)
        slot_tok, block_exp, nused, slot_of = _moe_meta(idx[:, :TOP_K], pos[:, :TOP_K], counts, MOE_TM)
        y_slots = _moe_experts(h, slot_tok, block_exp, nused, i, w_gu, b_gu, w_down, b_down)
        x = _combine(x, y_slots, slot_of, gates, modp, mods, final_g, seq, i == depth - 1)

    st = lambda name: jnp.stack(new[name])
    return (x[:n_p].reshape(bp, seq, D), x[n_p:].reshape(bs, tn_, D),
            st('rw_S_p'), st('rw_S_s'), st('sh_p'), st('sh_s'), st('ret_p'), st('ret_s'),
            st('k_p'), st('k_s'), st('v_p'), st('v_s'), st('lf_p'), st('lf_s'))
```

```python
import functools
import math

import numpy as np
import jax
import jax.numpy as jnp
from jax import lax
from jax.experimental import pallas as pl
from jax.experimental.pallas import tpu as pltpu

F32 = jnp.float32
BF16 = jnp.bfloat16
HI = lax.Precision.HIGHEST

D = 1024
NORM_EPS = 1e-6
A_W = 512
A_HEADS = 8
A_HD = 64
RWKV_PROJ = 1792
LNX_EPS = 64e-5
B_HEADS = 4
B_HD = 128
GN_EPS = 1e-5
C_HEADS = 16
C_HD = 64
N_EXPERTS = 32
TOP_K = 4
SWIGLU_LIMIT = 7.0
SWIGLU_ALPHA = 1.702

V7X_VMEM_BYTES = 64 * 1024 * 1024
VMEM_LIMIT = V7X_VMEM_BYTES - 8 * 1024 * 1024
LANES = 128
NEG = -1e30

TM = 512
MOE_TM = 512
MOE_LOOKAHEAD = 2
CMB_TM = 256
CMB_UNROLL = 8
RW_NSEQ = 8
RW_GROUPS = 2
RW_TC = 64
RET_L = 256
FOX_TQ = 512
FOX_PAGES = 8


def _cp(*sem):
    return pltpu.CompilerParams(dimension_semantics=sem, vmem_limit_bytes=VMEM_LIMIT)


def _sigmoid(x):
    return 1.0 / (1.0 + jnp.exp(-x))


def _softplus(x):
    return jnp.maximum(x, 0.0) + jnp.log(1.0 + jnp.exp(-jnp.abs(x)))


def _bdot(a, b):
    return jnp.dot(a.astype(BF16), b.astype(BF16), preferred_element_type=F32)


def _hdot(a, b):
    return jnp.dot(a, b, precision=HI, preferred_element_type=F32)


def _nt_dot(a, b):
    return lax.dot_general(a, b, (((1,), (1,)), ((), ())), preferred_element_type=F32)


def _rms(x, g):
    return x * lax.rsqrt(jnp.mean(x * x, axis=-1, keepdims=True) + NORM_EPS) * g


def _store_tiles(ref, x):
    for s in range(x.shape[1] // LANES):
        ref[:, s, :] = x[:, s * LANES:(s + 1) * LANES]


def _load_tiles(ref):
    return jnp.concatenate([ref[:, s, :] for s in range(ref.shape[1])], axis=1)


def _ada_kernel(c_ref, w_ref, b_ref, o_ref):
    c = c_ref[...]
    o_ref[...] = _bdot(c * _sigmoid(c), w_ref[...]) + b_ref[...]


def _ada(c_all, w_ada, b_ada):
    nl = w_ada.shape[0]
    m = c_all.shape[0]
    return pl.pallas_call(
        _ada_kernel,
        grid=(nl, 6),
        in_specs=[pl.BlockSpec((m, D), lambda l, j: (0, 0)),
                  pl.BlockSpec((None, D, D), lambda l, j: (l, 0, j)),
                  pl.BlockSpec((None, None, 1, D), lambda l, j: (l, j, 0, 0))],
        out_specs=pl.BlockSpec((None, None, m, D), lambda l, j: (l, j, 0, 0)),
        out_shape=jax.ShapeDtypeStruct((nl, 6, m, D), F32),
        compiler_params=_cp("arbitrary", "arbitrary"),
        name="ada",
    )(c_all, w_ada, b_ada.reshape(nl, 6, 1, D))


def _modp_spec(which, tm, seq, nb):
    tps = seq // tm
    return pl.BlockSpec((None, None, 1, D), lambda i, *_: (which, jnp.minimum(i // tps, nb - 1), 0, 0))


def _mods_spec(which, tm, npt):
    return pl.BlockSpec((None, tm, D), lambda i, *_: (which, jnp.maximum(i - npt, 0), 0))


def _inproj_kernel(x_ref, g_ref, sh, sc, w_ref, *rest, has_extra):
    if has_extra:
        we_ref, be_ref, o_ref, oe_ref, h_scr = rest
    else:
        o_ref, h_scr = rest

    @pl.when(pl.program_id(1) == 0)
    def _():
        h = (_rms(x_ref[...], g_ref[...]) * (1.0 + sc[...]) + sh[...]).astype(BF16)
        h_scr[...] = h
        if has_extra:
            f = jnp.dot(h, we_ref[...], preferred_element_type=F32) + be_ref[...]
            oe_ref[...] = jnp.minimum(f, 0.0) - jnp.log(1.0 + jnp.exp(-jnp.abs(f)))

    o_ref[...] = jnp.dot(h_scr[...], w_ref[...], preferred_element_type=F32)


def _inproj(x, g, mod, per_token, tile0, ntiles, seq, w, tn, extra=None):
    nj = w.shape[1] // tn
    rows = ntiles * TM
    has_extra = extra is not None
    if per_token:
        mspec = lambda which: pl.BlockSpec((None, TM, D), lambda i, j: (which, i, 0))
    else:
        tps = seq // TM
        mspec = lambda which: pl.BlockSpec((None, None, 1, D), lambda i, j: (which, i // tps, 0, 0))
    in_specs = [pl.BlockSpec((TM, D), lambda i, j: (tile0 + i, 0)),
                pl.BlockSpec((1, D), lambda i, j: (0, 0)),
                mspec(0), mspec(1),
                pl.BlockSpec((D, tn), lambda i, j: (0, j))]
    args = [x, g.reshape(1, D), mod, mod, w]
    out_specs = [pl.BlockSpec((None, TM, tn), lambda i, j: (j, i, 0))]
    out_shape = [jax.ShapeDtypeStruct((nj, rows, tn), F32)]
    if has_extra:
        we, be = extra
        in_specs += [pl.BlockSpec((D, LANES), lambda i, j: (0, 0)),
                     pl.BlockSpec((1, LANES), lambda i, j: (0, 0))]
        args += [we, be]
        out_specs.append(pl.BlockSpec((TM, LANES), lambda i, j: (i, 0)))
        out_shape.append(jax.ShapeDtypeStruct((rows, LANES), F32))
    return pl.pallas_call(
        functools.partial(_inproj_kernel, has_extra=has_extra),
        grid=(ntiles, nj),
        in_specs=in_specs, out_specs=out_specs, out_shape=out_shape,
        scratch_shapes=[pltpu.VMEM((TM, D), BF16)],
        compiler_params=_cp("arbitrary", "arbitrary"),
        name="inproj",
    )(*args)


def _outproj_kernel(*refs, nparts, npt):
    x_ref = refs[0]
    mixp = refs[1:1 + nparts]
    mixs = refs[1 + nparts:1 + 2 * nparts]
    ws = refs[1 + 2 * nparts:1 + 3 * nparts]
    (gp, gs, nfg, shp, scp, shs, scs, wr, br, xo, ho, go, io, po, co, cnt_s) = refs[1 + 3 * nparts:]
    i = pl.program_id(0)
    isp = i < npt

    @pl.when(i == 0)
    def _():
        cnt_s[...] = jnp.zeros(cnt_s.shape, F32)

    acc = None
    for mp, ms, w in zip(mixp, mixs, ws):
        part = _bdot(jnp.where(isp, mp[...], ms[...]), w[...])
        acc = part if acc is None else acc + part
    xn = x_ref[...] + jnp.where(isp, gp[...], gs[...]) * acc
    xo[...] = xn
    y = _rms(xn, nfg[...])
    h = y * (1.0 + jnp.where(isp, scp[...], scs[...])) + jnp.where(isp, shp[...], shs[...])
    _store_tiles(ho, h)
    logits = _hdot(h, wr[...]) + br[...]
    tm = logits.shape[0]
    lane = lax.broadcasted_iota(jnp.int32, logits.shape, 1)
    l = logits
    vals, idxs = [], []
    for _ in range(TOP_K):
        m = jnp.max(l, axis=-1, keepdims=True)
        ix = jnp.min(jnp.where(l == m, lane, LANES), axis=-1, keepdims=True)
        vals.append(m)
        idxs.append(ix)
        l = jnp.where(lane == ix, -jnp.inf, l)
    es = [jnp.exp(v - vals[0]) for v in vals]
    den = es[0] + es[1] + es[2] + es[3]
    dense = jnp.zeros(logits.shape, F32)
    for k in range(TOP_K):
        dense = jnp.where(lane == idxs[k], 1.0, dense)
    before = (lax.broadcasted_iota(jnp.int32, (tm, tm), 0) > lax.broadcasted_iota(jnp.int32, (tm, tm), 1))
    rank = jnp.dot(jnp.where(before, 1.0, 0.0).astype(BF16), dense.astype(BF16),
                   preferred_element_type=F32) + cnt_s[...]
    cnt = cnt_s[...] + jnp.sum(dense, axis=0, keepdims=True)
    cnt_s[...] = cnt
    co[...] = cnt
    gates = jnp.zeros(logits.shape, F32)
    idxo = jnp.zeros(logits.shape, jnp.int32)
    poso = jnp.zeros(logits.shape, jnp.int32)
    for k in range(TOP_K):
        pk = jnp.sum(jnp.where(lane == idxs[k], rank, 0.0), axis=-1, keepdims=True)
        gates = jnp.where(lane == k, es[k] / den, gates)
        idxo = jnp.where(lane == k, idxs[k], idxo)
        poso = jnp.where(lane == k, pk.astype(jnp.int32), poso)
    go[...] = gates
    io[...] = idxo
    po[...] = poso


def _outproj(x, mixes_p, mixes_s, ws, modp, mods, nfg, wr, br, seq):
    n = x.shape[0]
    n_s = mods.shape[1]
    npt = (n - n_s) // TM
    nb = modp.shape[1]
    nparts = len(ws)
    row = lambda i: (i, 0)
    const = lambda i: (0, 0)
    in_specs = [pl.BlockSpec((TM, D), row)]
    in_specs += [pl.BlockSpec((TM, m.shape[1]), lambda i: (jnp.minimum(i, npt - 1), 0)) for m in mixes_p]
    in_specs += [pl.BlockSpec((TM, m.shape[1]), lambda i: (jnp.maximum(i - npt, 0), 0)) for m in mixes_s]
    in_specs += [pl.BlockSpec(w.shape, const) for w in ws]
    in_specs += [_modp_spec(2, TM, seq, nb), _mods_spec(2, TM, npt),
                 pl.BlockSpec((1, D), const),
                 _modp_spec(3, TM, seq, nb), _modp_spec(4, TM, seq, nb),
                 _mods_spec(3, TM, npt), _mods_spec(4, TM, npt),
                 pl.BlockSpec((D, LANES), const), pl.BlockSpec((1, LANES), const)]
    args = [x, *mixes_p, *mixes_s, *ws, modp, mods, nfg.reshape(1, D), modp, modp, mods, mods, wr, br]
    return pl.pallas_call(
        functools.partial(_outproj_kernel, nparts=nparts, npt=npt),
        grid=(n // TM,),
        in_specs=in_specs,
        out_specs=[pl.BlockSpec((TM, D), row), pl.BlockSpec((TM, D // LANES, LANES), lambda i: (i, 0, 0)),
                   pl.BlockSpec((TM, LANES), row), pl.BlockSpec((TM, LANES), row),
                   pl.BlockSpec((TM, LANES), row), pl.BlockSpec((1, LANES), const)],
        out_shape=[jax.ShapeDtypeStruct((n, D), F32), jax.ShapeDtypeStruct((n, D // LANES, LANES), F32),
                   jax.ShapeDtypeStruct((n, LANES), F32), jax.ShapeDtypeStruct((n, LANES), jnp.int32),
                   jax.ShapeDtypeStruct((n, LANES), jnp.int32), jax.ShapeDtypeStruct((1, LANES), F32)],
        scratch_shapes=[pltpu.VMEM((1, LANES), F32)],
        compiler_params=_cp("arbitrary"),
        name="outproj",
    )(*args)


def _moe_kernel(tok_ref, bexp_ref, nused_ref, h_hbm, wgu_ref, bgu_ref, wd_ref, bd_ref,
                o_ref, *rest, tm):
    nbuf = MOE_LOOKAHEAD + 1
    bufs = rest[:nbuf]
    wgu_s, wd_s, sem = rest[nbuf:]
    blk = pl.program_id(0)
    nblk = pl.num_programs(0)
    nu = nused_ref[0]

    def row_copy(b, r, buf, s):
        return pltpu.make_async_copy(h_hbm.at[pl.ds(tok_ref[b * tm + r], 1)], buf.at[pl.ds(r, 1)], sem.at[s])

    def issue_loop(b, buf, s):
        def body(r, c):
            row_copy(b, r, buf, s).start()
            return c
        lax.fori_loop(0, tm, body, 0)

    @pl.when(blk == 0)
    def _():
        for b in range(MOE_LOOKAHEAD):
            issue_loop(b, bufs[b], b)

    def step(cur, cs, nxt, ns):
        pltpu.make_async_copy(h_hbm.at[pl.ds(0, tm)], cur, sem.at[cs]).wait()

        @pl.when(blk < nu)
        def _():
            e = bexp_ref[blk]
            e_prev = bexp_ref[jnp.maximum(blk - 1, 0)]

            @pl.when((blk == 0) | (e != e_prev))
            def _():
                wgu_s[...] = wgu_ref[...].astype(BF16)
                wd_s[...] = wd_ref[...].astype(BF16)

            for r in range(tm):
                row_copy(blk + MOE_LOOKAHEAD, r, nxt, ns).start()
            xb = _load_tiles(cur).astype(BF16)
            gu = jnp.dot(xb, wgu_s[...], preferred_element_type=F32) + bgu_ref[...]
            gate = jnp.minimum(gu[:, :D], SWIGLU_LIMIT)
            up = jnp.clip(gu[:, D:], -SWIGLU_LIMIT, SWIGLU_LIMIT)
            act = (up + 1.0) * gate * _sigmoid(SWIGLU_ALPHA * gate)
            o_ref[...] = jnp.dot(act.astype(BF16), wd_s[...], preferred_element_type=F32) + bd_ref[...]

        @pl.when(blk >= nu)
        def _():
            @pl.when(blk + MOE_LOOKAHEAD < nblk)
            def _():
                issue_loop(blk + MOE_LOOKAHEAD, nxt, ns)
            o_ref[...] = jnp.zeros(o_ref.shape, F32)

    for k in range(nbuf):
        ahead = (k + MOE_LOOKAHEAD) % nbuf
        pl.when(blk % nbuf == k)(functools.partial(step, bufs[k], k, bufs[ahead], ahead))


def _moe_experts(h, slot_tok, block_exp, nused, layer, w_gu, b_gu, w_down, b_down):
    tm = MOE_TM
    nb = block_exp.shape[0]
    nl = w_gu.shape[0]
    grid_spec = pltpu.PrefetchScalarGridSpec(
        num_scalar_prefetch=3,
        grid=(nb,),
        in_specs=[pl.BlockSpec(memory_space=pl.ANY),
                  pl.BlockSpec((None, None, D, 2 * D), lambda b, t, e, u: (layer, e[b], 0, 0)),
                  pl.BlockSpec((None, None, 1, 2 * D), lambda b, t, e, u: (layer, e[b], 0, 0)),
                  pl.BlockSpec((None, None, D, D), lambda b, t, e, u: (layer, e[b], 0, 0)),
                  pl.BlockSpec((None, None, 1, D), lambda b, t, e, u: (layer, e[b], 0, 0))],
        out_specs=pl.BlockSpec((tm, D), lambda b, t, e, u: (b, 0)),
        scratch_shapes=[pltpu.VMEM((tm, D // LANES, LANES), F32) for _ in range(MOE_LOOKAHEAD + 1)]
        + [pltpu.VMEM((D, 2 * D), BF16),
           pltpu.VMEM((D, D), BF16),
           pltpu.SemaphoreType.DMA((MOE_LOOKAHEAD + 1,))])
    return pl.pallas_call(
        functools.partial(_moe_kernel, tm=tm),
        grid_spec=grid_spec,
        out_shape=jax.ShapeDtypeStruct((nb * tm, D), F32),
        compiler_params=_cp("arbitrary"),
        name="moe_experts",
    )(slot_tok, block_exp, nused, h, w_gu,
      b_gu.reshape(nl, N_EXPERTS, 1, 2 * D), w_down, b_down.reshape(nl, N_EXPERTS, 1, D))


def _combine_kernel(sof_ref, y_hbm, x_ref, gt_ref, gp, gs, fg_ref, o_ref, y0, y1, sem, *, tm, npt, final):
    i = pl.program_id(0)
    nt = pl.num_programs(0)

    def issue(t, buf, s, unroll):
        base = t * (tm * TOP_K)

        def body(r, c):
            for k in range(TOP_K):
                slot = sof_ref[base + r * TOP_K + k]
                pltpu.make_async_copy(y_hbm.at[pl.ds(slot, 1)], buf.at[k, pl.ds(r, 1)], sem.at[s]).start()
            return c

        lax.fori_loop(0, tm, body, 0, unroll=unroll)

    @pl.when(i == 0)
    def _():
        issue(0, y0, 0, 4)

    def step(cur, cs, nxt, ns):
        @pl.when(i + 1 < nt)
        def _():
            issue(i + 1, nxt, ns, CMB_UNROLL)

        for k in range(TOP_K):
            pltpu.make_async_copy(y_hbm.at[pl.ds(0, tm)], cur.at[k], sem.at[cs]).wait()
        gt = gt_ref[...]
        y = ((cur[0] * gt[:, 0:1] + cur[1] * gt[:, 1:2]) + (cur[2] * gt[:, 2:3] + cur[3] * gt[:, 3:4]))
        xn = x_ref[...] + jnp.where(i < npt, gp[...], gs[...]) * y
        if final:
            xn = _rms(xn, fg_ref[...])
        o_ref[...] = xn

    pl.when(i % 2 == 0)(functools.partial(step, y0, 0, y1, 1))
    pl.when(i % 2 == 1)(functools.partial(step, y1, 1, y0, 0))


def _combine(x, y_slots, slot_of, gates, modp, mods, final_g, seq, final):
    tm = CMB_TM
    n = x.shape[0]
    n_s = mods.shape[1]
    npt = (n - n_s) // tm
    nb = modp.shape[1]
    grid_spec = pltpu.PrefetchScalarGridSpec(
        num_scalar_prefetch=1,
        grid=(n // tm,),
        in_specs=[pl.BlockSpec(memory_space=pl.ANY),
                  pl.BlockSpec((tm, D), lambda i, s: (i, 0)),
                  pl.BlockSpec((tm, LANES), lambda i, s: (i, 0)),
                  _modp_spec(5, tm, seq, nb), _mods_spec(5, tm, npt),
                  pl.BlockSpec((1, D), lambda i, s: (0, 0))],
        out_specs=pl.BlockSpec((tm, D), lambda i, s: (i, 0)),
        scratch_shapes=[pltpu.VMEM((TOP_K, tm, D), F32), pltpu.VMEM((TOP_K, tm, D), F32),
                        pltpu.SemaphoreType.DMA((2,))])
    return pl.pallas_call(
        functools.partial(_combine_kernel, tm=tm, npt=npt, final=final),
        grid_spec=grid_spec,
        out_shape=jax.ShapeDtypeStruct((n, D), F32),
        compiler_params=_cp("arbitrary"),
        name="moe_combine",
    )(slot_of, y_slots, x, gates, modp, mods, final_g.reshape(1, D))


def _moe_meta(idx4, pos4, counts, tm):
    n = idx4.shape[0]
    nk = n * TOP_K
    nb = -(-(nk + N_EXPERTS * (tm - 1)) // tm) + MOE_LOOKAHEAD
    padded = (counts + tm - 1) // tm * tm
    pad_end = jnp.cumsum(padded)
    pad_start = pad_end - padded
    flat = (pad_start[idx4] + pos4).astype(jnp.int32).reshape(nk)
    tok = jnp.repeat(jnp.arange(n, dtype=jnp.int32), TOP_K)
    slot_tok = jnp.zeros((nb * tm,), jnp.int32).at[flat].set(tok, unique_indices=True)
    starts = jnp.arange(nb, dtype=jnp.int32) * tm
    block_exp = jnp.minimum(jnp.sum((pad_end[None, :] <= starts[:, None]).astype(jnp.int32), axis=1),
                            N_EXPERTS - 1).astype(jnp.int32)
    nused = (pad_end[-1] // tm).astype(jnp.int32).reshape(1)
    return slot_tok, block_exp, nused, flat


def _seg_consts():
    r = np.arange(256)
    mseg = (r[:, None] // A_HD == r[None, :] // A_HD).astype(np.float32)
    c = np.arange(A_W)
    eyet = (np.arange(A_HD)[:, None] == (c[None, :] % A_HD)).astype(np.float32)
    return jnp.asarray(mseg, BF16), jnp.asarray(eyet, F32)


def _split(x):
    hi = x.astype(BF16)
    lo = (x - hi.astype(F32)).astype(BF16)
    return hi, lo


def _segsum(x, mseg):
    hi, lo = _split(x)
    halves = []
    for hf in range(2):
        sl = slice(hf * 256, (hf + 1) * 256)
        halves.append(jnp.dot(hi[:, sl], mseg, preferred_element_type=F32)
                      + jnp.dot(lo[:, sl], mseg, preferred_element_type=F32))
    return jnp.concatenate(halves, axis=1)


def _rwkv_kernel(*refs, nseq, tc, has_vres, ngrp):
    it = iter(refs)
    p_ref, st_ref, s0_ref = next(it), next(it), next(it)
    vf_ref = next(it) if has_vres else None
    mu, w0, w2, a0, a2, g2, kkp, kap, rk, lng, lnb = (next(it) for _ in range(11))
    if has_vres:
        v0, v1, v2 = next(it), next(it), next(it)
    mseg_ref, eyet_ref = next(it), next(it)
    o_ref, sout_ref, shout_ref = next(it), next(it), next(it)
    vfo_ref = None if has_vres else next(it)
    w_s, k_s, v_s, nkk_s, b_s, r_s, out_s, st_s, prev_s = (next(it) for _ in range(9))

    c = pl.program_id(1)
    nch = pl.num_programs(1)
    rows = nseq * tc
    mseg = mseg_ref[...]
    mseg2 = jnp.concatenate([mseg, mseg], axis=0)
    eyet = eyet_ref[...]

    @pl.when(c == 0)
    def _():
        prev_s[...] = st_ref[...]
        for s in range(nseq):
            for h in range(A_HEADS):
                st_s[s, :, h * A_HD:(h + 1) * A_HD] = s0_ref[s, h]

    p = p_ref[:, :, :RWKV_PROJ].reshape(rows, RWKV_PROJ)
    rolled = pltpu.roll(p, 1, 0)
    ridx = lax.broadcasted_iota(jnp.int32, (rows, 1), 0)
    sidx = lax.broadcasted_iota(jnp.int32, (rows, nseq), 1)
    onehot = (lax.broadcasted_iota(jnp.int32, (rows, nseq), 0) == sidx * tc).astype(F32)
    prev = jnp.where(ridx % tc == 0, _hdot(onehot, prev_s[...]), rolled)
    z = p + (prev - p) * mu[...]
    r = z[:, 0:512]
    k = z[:, 512:1024]
    v = z[:, 1024:1536]
    wi = z[:, 1536:1600]
    ai = z[:, 1600:1664]
    gi = z[:, 1664:1792]
    w_log = -_softplus(-(w0[...] + _hdot(jnp.tanh(wi), w2[...]))) - 0.5
    w = jnp.exp(-jnp.exp(w_log))
    a = _sigmoid(a0[...] + _hdot(ai, a2[...]))
    g = _bdot(_sigmoid(gi), g2[...])
    if has_vres:
        vf = vf_ref[...].reshape(rows, A_W)
        v = v + (vf - v) * _sigmoid(v0[...] + _bdot(_bdot(v, v1[...]), v2[...]))
    else:
        vfo_ref[...] = v.reshape(nseq, tc, A_W)
    kk = k * kkp[...]
    kk = kk * lax.rsqrt(jnp.maximum(_segsum(kk * kk, mseg), 1e-24))
    k = k * (1.0 + (a - 1.0) * kap[...])
    w_s[...] = w
    k_s[...] = k
    v_s[...] = v
    nkk_s[...] = -kk
    b_s[...] = kk * a
    r_s[...] = r
    bonus = _segsum(r * k * rk[...], mseg) * v

    def halves(x):
        return [x[:, 0:256], x[:, 256:512]]

    def token(t, carry):
        groups = [range(g0, g0 + nseq // ngrp) for g0 in range(0, nseq, nseq // ngrp)]
        ress = []
        for grp in groups:
            lhs, lhv = [], []
            for s in grp:
                row = s * tc + t
                x1h, x1l = _split(st_s[s] * nkk_s[pl.ds(row, 1), :])
                lhs += [jnp.concatenate([a, b], axis=1) for a, b in zip(halves(x1h), halves(x1l))]
                lhv += halves((eyet * v_s[pl.ds(row, 1), :]).astype(BF16))
            ress.append((jnp.dot(jnp.concatenate(lhs, axis=0), mseg2, preferred_element_type=F32),
                         jnp.dot(jnp.concatenate(lhv, axis=0), mseg, preferred_element_type=F32)))
        ress2 = []
        for grp, (res, resv) in zip(groups, ress):
            lhs2 = []
            for n, s in enumerate(grp):
                row = s * tc + t
                o = n * 2 * A_HD
                sa = jnp.concatenate([res[o:o + A_HD], res[o + A_HD:o + 2 * A_HD]], axis=1)
                v2 = jnp.concatenate([resv[o:o + A_HD], resv[o + A_HD:o + 2 * A_HD]], axis=1)
                st = (st_s[s] * w_s[pl.ds(row, 1), :] + sa * b_s[pl.ds(row, 1), :]
                      + v2 * k_s[pl.ds(row, 1), :])
                st_s[s] = st
                lhs2 += halves((st * r_s[pl.ds(row, 1), :]).astype(BF16))
            ress2.append(jnp.dot(jnp.concatenate(lhs2, axis=0), mseg, preferred_element_type=F32))
        for grp, res2 in zip(groups, ress2):
            for n, s in enumerate(grp):
                row = s * tc + t
                o = n * 2 * A_HD
                o2 = jnp.concatenate([res2[o:o + A_HD], res2[o + A_HD:o + 2 * A_HD]], axis=1)
                out_s[pl.ds(row, 1), :] = jnp.sum(o2 * eyet, axis=0, keepdims=True)
        return carry

    lax.fori_loop(0, tc, token, 0)

    out = out_s[...]
    xc = out - _segsum(out, mseg) * (1.0 / A_HD)
    var = _segsum(xc * xc, mseg) * (1.0 / A_HD)
    y = xc * lax.rsqrt(var + LNX_EPS) * lng[...] + lnb[...]
    o_ref[...] = ((y + bonus) * g).reshape(nseq, tc, A_W)

    for s in range(nseq):
        prev_s[s:s + 1, :] = p[s * tc + tc - 1:s * tc + tc, :]

    @pl.when(c == nch - 1)
    def _():
        shout_ref[...] = prev_s[...]
        for s in range(nseq):
            for h in range(A_HEADS):
                sout_ref[s, h] = st_s[s, :, h * A_HD:(h + 1) * A_HD]


def _rwkv(proj, nb, seq, nseq, tc, st_shift, s0, vfirst, lp, vres):
    nch = seq // tc
    rows = nseq * tc
    has_vres = vres is not None
    blk3 = lambda b, c: (b, c, 0)
    vec = lambda a: a.reshape(1, -1)
    c2 = lambda b, c: (0, 0)
    mseg, eyet = _seg_consts()
    in_specs = [pl.BlockSpec((None, nseq, tc, 2048), lambda b, c: (0, b, c, 0)),
                pl.BlockSpec((None, nseq, RWKV_PROJ), lambda b, c: (b, 0, 0)),
                pl.BlockSpec((nseq, A_HEADS, A_HD, A_HD), lambda b, c: (b, 0, 0, 0))]
    args = [proj.reshape(proj.shape[0], nb, seq, 2048), st_shift.reshape(nb // nseq, nseq, RWKV_PROJ), s0]
    if has_vres:
        in_specs.append(pl.BlockSpec((nseq, tc, A_W), blk3))
        args.append(vfirst)
    small = [vec(lp['mu']), vec(lp['w0']), lp['w2'], vec(lp['a0']), lp['a2'], lp['g2'].astype(BF16),
             vec(lp['k_k']), vec(lp['k_a']), vec(lp['r_k']), vec(lp['ln_g']), vec(lp['ln_b'])]
    if has_vres:
        small += [vec(vres[0]), vres[1].astype(BF16), vres[2].astype(BF16)]
    small += [mseg, eyet]
    in_specs += [pl.BlockSpec(a.shape, c2) for a in small]
    args += small
    out_specs = [pl.BlockSpec((nseq, tc, A_W), blk3),
                 pl.BlockSpec((nseq, A_HEADS, A_HD, A_HD), lambda b, c: (b, 0, 0, 0)),
                 pl.BlockSpec((None, nseq, RWKV_PROJ), lambda b, c: (b, 0, 0))]
    out_shape = [jax.ShapeDtypeStruct((nb, seq, A_W), F32),
                 jax.ShapeDtypeStruct((nb, A_HEADS, A_HD, A_HD), F32),
                 jax.ShapeDtypeStruct((nb // nseq, nseq, RWKV_PROJ), F32)]
    if not has_vres:
        out_specs.append(pl.BlockSpec((nseq, tc, A_W), blk3))
        out_shape.append(jax.ShapeDtypeStruct((nb, seq, A_W), F32))
    scratch = [pltpu.VMEM((rows, A_W), F32) for _ in range(7)]
    scratch += [pltpu.VMEM((nseq, A_HD, A_W), F32), pltpu.VMEM((nseq, RWKV_PROJ), F32)]
    outs = pl.pallas_call(
        functools.partial(_rwkv_kernel, nseq=nseq, tc=tc, has_vres=has_vres, ngrp=RW_GROUPS),
        grid=(nb // nseq, nch),
        in_specs=in_specs, out_specs=out_specs, out_shape=out_shape,
        scratch_shapes=scratch,
        compiler_params=_cp("arbitrary", "arbitrary"),
        name="rwkv",
    )(*args)
    o, s_out, sh_out = outs[0].reshape(nb * seq, A_W), outs[1], outs[2].reshape(nb, RWKV_PROJ)
    vf_out = vfirst if has_vres else outs[3]
    return o, s_out, sh_out, vf_out


def _ret_kernel(pr_ref, s0_ref, cos_ref, sin_ref, gn_ref, o_ref, sout_ref, st_s, *, nseq, L):
    c = pl.program_id(1)
    nch = pl.num_programs(1)

    @pl.when(c == 0)
    def _():
        st_s[...] = s0_ref[...]

    cos = cos_ref[...]
    sin = sin_ref[...]
    ii = lax.broadcasted_iota(jnp.int32, (L, L), 0)
    jj = lax.broadcasted_iota(jnp.int32, (L, L), 1)
    dif = (ii - jj).astype(F32)
    ri = lax.broadcasted_iota(jnp.int32, (L, 1), 0).astype(F32)

    def rot(x):
        return x * cos + pltpu.roll(x, B_HD // 2, 1) * sin

    for h in range(B_HEADS):
        lg = math.log1p(-2.0 ** (-5.0 - h))
        intra = jnp.where(dif >= 0, jnp.exp(jnp.maximum(dif, 0.0) * lg), 0.0)
        q_dec = jnp.exp((ri + 1.0) * lg)
        k_dec = jnp.exp((L - 1.0 - ri) * lg)
        c_dec = math.exp(L * lg)
        gn = gn_ref[:, h * B_HD:(h + 1) * B_HD]
        for s in range(nseq):
            rs = slice(s * L, (s + 1) * L)
            q = rot(pr_ref[rs, h * B_HD:(h + 1) * B_HD])
            k = rot(pr_ref[rs, 512 + h * B_HD:512 + (h + 1) * B_HD]) * (B_HD ** -0.5)
            v = pr_ref[rs, 1024 + h * B_HD:1024 + (h + 1) * B_HD]
            g = pr_ref[rs, 1536 + h * B_HD:1536 + (h + 1) * B_HD]
            st = st_s[s, h]
            vb = v.astype(BF16)
            att = _nt_dot(q.astype(BF16), k.astype(BF16)) * intra
            o = (jnp.dot(att.astype(BF16), vb, preferred_element_type=F32)
                 + _bdot(q * q_dec, st))
            st_s[s, h] = st * c_dec + lax.dot_general(
                (k * k_dec).astype(BF16), vb, (((0,), (0,)), ((), ())), preferred_element_type=F32)
            oc = o - jnp.mean(o, axis=-1, keepdims=True)
            y = oc * lax.rsqrt(jnp.mean(oc * oc, axis=-1, keepdims=True) + GN_EPS) * gn
            o_ref[rs, h * B_HD:(h + 1) * B_HD] = g * _sigmoid(g) * y

    @pl.when(c == nch - 1)
    def _():
        sout_ref[...] = st_s[...]


def _rot_tables(pos0, t):
    half = B_HD // 2
    inv = 1.0 / (10000.0 ** jnp.linspace(0.0, 1.0, half, dtype=F32))
    ang = (pos0 + jnp.arange(t, dtype=F32))[:, None] * inv[None, :]
    cos, sin = jnp.cos(ang), jnp.sin(ang)
    return jnp.concatenate([cos, cos], axis=1), jnp.concatenate([-sin, sin], axis=1)


def _retention(proj, nb, seq, nseq, L, pos0, s0, gn_g):
    n = proj.shape[1]
    nch = seq // L
    rows = nseq * L
    cos, sin = _rot_tables(float(pos0), seq)
    rowmap = lambda b, c: (b * nch + c, 0)
    return pl.pallas_call(
        functools.partial(_ret_kernel, nseq=nseq, L=L),
        grid=(nb // nseq, nch),
        in_specs=[pl.BlockSpec((None, rows, 2048), lambda b, c: (1, b * nch + c, 0)),
                  pl.BlockSpec((nseq, B_HEADS, B_HD, B_HD), lambda b, c: (b, 0, 0, 0)),
                  pl.BlockSpec((L, B_HD), lambda b, c: (c, 0)),
                  pl.BlockSpec((L, B_HD), lambda b, c: (c, 0)),
                  pl.BlockSpec((1, 512), lambda b, c: (0, 0))],
        out_specs=[pl.BlockSpec((rows, 512), rowmap),
                   pl.BlockSpec((nseq, B_HEADS, B_HD, B_HD), lambda b, c: (b, 0, 0, 0))],
        out_shape=[jax.ShapeDtypeStruct((n, 512), F32),
                   jax.ShapeDtypeStruct((nb, B_HEADS, B_HD, B_HD), F32)],
        scratch_shapes=[pltpu.VMEM((nseq, B_HEADS, B_HD, B_HD), F32)],
        compiler_params=_cp("arbitrary", "arbitrary"),
        name="retention",
    )(proj, s0, cos, sin, gn_g.reshape(1, 512))


def _tri(n):
    return (lax.broadcasted_iota(jnp.int32, (n, n), 0) >= lax.broadcasted_iota(jnp.int32, (n, n), 1)).astype(F32)


def _cumsum_kernel(lf_ref, o_ref):
    t = lf_ref.shape[0]
    tri = _tri(LANES)
    carry = jnp.zeros((1, LANES), F32)
    for b in range(t // LANES):
        rs = slice(b * LANES, (b + 1) * LANES)
        cb = _hdot(tri, lf_ref[rs, :]) + carry
        o_ref[rs, :] = cb
        carry = cb[LANES - 1:LANES, :]


def _cumsum_prompt(logf, nb, seq):
    return pl.pallas_call(
        _cumsum_kernel,
        grid=(nb,),
        in_specs=[pl.BlockSpec((seq, LANES), lambda b: (b, 0))],
        out_specs=pl.BlockSpec((seq, LANES), lambda b: (b, 0)),
        out_shape=jax.ShapeDtypeStruct((nb * seq, LANES), F32),
        compiler_params=_cp("arbitrary"),
        name="fox_cumsum",
    )(logf)


def _foxp_kernel(q_ref, k_ref, v_ref, cq_ref, ck_ref, o_ref, *, tq):
    hp = pl.program_id(1)
    qi = pl.program_id(2)
    lane = lax.broadcasted_iota(jnp.int32, (tq, LANES), 1)
    qs, cqs = [], []
    for hh in range(2):
        hs = slice(hh * C_HD, (hh + 1) * C_HD)
        qs.append((q_ref[:, hs] * (C_HD ** -0.5)).astype(BF16))
        cqs.append(jnp.sum(jnp.where(lane == hp * 2 + hh, cq_ref[...], 0.0), axis=-1, keepdims=True))

    def block(off, diag, carry):
        new = []
        for hh in range(2):
            m, l, acc = carry[3 * hh:3 * hh + 3]
            hs = slice(hh * C_HD, (hh + 1) * C_HD)
            k = k_ref[pl.ds(off, tq), hs].astype(BF16)
            v = v_ref[pl.ds(off, tq), hs].astype(BF16)
            s = _nt_dot(qs[hh], k) + (cqs[hh] - ck_ref[pl.ds(hp * 2 + hh, 1), pl.ds(off, tq)])
            if diag:
                s = jnp.where(lax.broadcasted_iota(jnp.int32, (tq, tq), 0)
                              >= lax.broadcasted_iota(jnp.int32, (tq, tq), 1), s, NEG)
            m_new = jnp.maximum(m, jnp.max(s, axis=-1, keepdims=True))
            alpha = jnp.exp(m - m_new)
            p = jnp.exp(s - m_new)
            new += [m_new, alpha * l + jnp.sum(p, axis=-1, keepdims=True),
                    alpha * acc + jnp.dot(p.astype(BF16), v, preferred_element_type=F32)]
        return tuple(new)

    init = (jnp.full((tq, 1), NEG, F32), jnp.zeros((tq, 1), F32), jnp.zeros((tq, C_HD), F32)) * 2
    carry = lax.fori_loop(0, qi, lambda kj, c: block(pl.multiple_of(kj * tq, tq), False, c), init)
    carry = block(pl.multiple_of(qi * tq, tq), True, carry)
    o_ref[...] = jnp.concatenate([carry[2] / carry[1], carry[5] / carry[4]], axis=1)


def _fox_prompt(qkv, c, ct, nb, seq):
    n = qkv.shape[1]
    tq = FOX_TQ
    nq = seq // tq
    return pl.pallas_call(
        functools.partial(_foxp_kernel, tq=tq),
        grid=(nb, C_HEADS // 2, nq),
        in_specs=[pl.BlockSpec((None, tq, LANES), lambda b, hp, qi: (0, b * nq + qi, hp)),
                  pl.BlockSpec((None, seq, LANES), lambda b, hp, qi: (1, b, hp)),
                  pl.BlockSpec((None, seq, LANES), lambda b, hp, qi: (2, b, hp)),
                  pl.BlockSpec((tq, LANES), lambda b, hp, qi: (b * nq + qi, 0)),
                  pl.BlockSpec((None, C_HEADS, seq), lambda b, hp, qi: (b, 0, 0))],
        out_specs=pl.BlockSpec((tq, LANES), lambda b, hp, qi: (b * nq + qi, hp)),
        out_shape=jax.ShapeDtypeStruct((n, D), F32),
        compiler_params=_cp("arbitrary", "arbitrary", "arbitrary"),
        name="fox_prompt",
    )(qkv, qkv, qkv, c, ct)


def _foxs_kernel(pt_ref, q_ref, kn_ref, vn_ref, lfn_ref, *rest, t_new, npp):
    kts, vts, clfs = rest[0:npp], rest[npp:2 * npp], rest[2 * npp:3 * npp]
    o_ref, qbd, cnq_s, m_s, l_s, acc_s, carry = rest[3 * npp:]
    g = pl.program_id(1)
    nrow = C_HEADS * t_new
    page = clfs[0].shape[1]
    rowh = lax.broadcasted_iota(jnp.int32, (nrow, D), 0) // t_new
    bmask = (lax.broadcasted_iota(jnp.int32, (nrow, D), 1) // C_HD) == rowh

    def per_row(x):
        return jnp.concatenate([jnp.broadcast_to(x[h:h + 1, :], (t_new, x.shape[1])) for h in range(C_HEADS)],
                               axis=0)

    def update(s, pv):
        m = m_s[...]
        m_new = jnp.maximum(m, jnp.max(s, axis=-1, keepdims=True))
        alpha = jnp.exp(m - m_new)
        p = jnp.exp(s - m_new)
        l_s[...] = alpha * l_s[...] + jnp.sum(p, axis=-1, keepdims=True)
        acc_s[...] = alpha * acc_s[...] + pv(p)
        m_s[...] = m_new

    @pl.when(g == 0)
    def _():
        q = q_ref[...] * (C_HD ** -0.5)
        qbd[...] = jnp.where(bmask, jnp.concatenate([q] * C_HEADS, axis=0), 0.0).astype(BF16)
        m_s[...] = jnp.full(m_s.shape, NEG, F32)
        l_s[...] = jnp.zeros(l_s.shape, F32)
        acc_s[...] = jnp.zeros(acc_s.shape, F32)
        carry[...] = jnp.zeros(carry.shape, F32)
        tle = (lax.broadcasted_iota(jnp.int32, (t_new, t_new), 0)
               <= lax.broadcasted_iota(jnp.int32, (t_new, t_new), 1)).astype(F32)
        cnt = lax.dot_general(lfn_ref[:, :C_HEADS], tle, (((0,), (0,)), ((), ())),
                              precision=HI, preferred_element_type=F32)
        gk = per_row(cnt)
        colt = lax.broadcasted_iota(jnp.int32, (nrow, t_new), 1)
        qrow = lax.broadcasted_iota(jnp.int32, (nrow, t_new), 0) % t_new
        cnq = jnp.sum(jnp.where(colt == qrow, gk, 0.0), axis=-1, keepdims=True)
        cnq_s[...] = cnq
        s2 = _nt_dot(qbd[...], kn_ref[...].astype(BF16)) + cnq - gk
        update(jnp.where(colt <= qrow, s2, NEG),
               lambda p: jnp.dot(p, vn_ref[...], preferred_element_type=F32))

    later = (lax.broadcasted_iota(jnp.int32, (page, page), 0)
             > lax.broadcasted_iota(jnp.int32, (page, page), 1)).astype(F32)
    lfs = [r[...] for r in clfs]
    suf = _hdot(jnp.concatenate(lfs, axis=0), later)
    run = carry[...]
    ds = []
    for i in range(npp):
        si = suf[i * C_HEADS:(i + 1) * C_HEADS]
        ds.append(si + run)
        run = run + si[:, 0:1] + lfs[i][:, 0:1]
    carry[...] = run
    qb = qbd[...]
    s = jnp.concatenate([jnp.dot(qb, kt[...].astype(BF16), preferred_element_type=F32) for kt in kts], axis=1)
    s = s + (per_row(jnp.concatenate(ds, axis=1)) + cnq_s[...])

    def pv(p):
        pb = p.astype(BF16)
        out = _nt_dot(pb[:, 0:page], vts[0][...].astype(BF16))
        for i in range(1, npp):
            out = out + _nt_dot(pb[:, i * page:(i + 1) * page], vts[i][...].astype(BF16))
        return out

    update(s, pv)

    @pl.when(g == pl.num_programs(1) - 1)
    def _():
        o = jnp.where(bmask, acc_s[...] / l_s[...], 0.0)
        o_ref[...] = jnp.sum(o.reshape(C_HEADS, t_new, D), axis=0)


def _fox_sample(qkv, logf, nb, t_new, page_table, layer, kt, vt, clft):
    n = qkv.shape[1]
    npg = page_table.shape[1]
    page = kt.shape[3]
    nrow = C_HEADS * t_new
    npp = FOX_PAGES

    def pidx(i):
        return lambda b, g, pt: (layer, pt[b * npg + (npg - 1 - (g * npp + i))], 0, 0)

    in_specs = [pl.BlockSpec((None, t_new, D), lambda b, g, pt: (0, b, 0)),
                pl.BlockSpec((None, t_new, D), lambda b, g, pt: (1, b, 0)),
                pl.BlockSpec((None, t_new, D), lambda b, g, pt: (2, b, 0)),
                pl.BlockSpec((t_new, LANES), lambda b, g, pt: (b, 0))]
    in_specs += [pl.BlockSpec((None, None, D, page), pidx(i)) for i in range(npp)]
    in_specs += [pl.BlockSpec((None, None, D, page), pidx(i)) for i in range(npp)]
    in_specs += [pl.BlockSpec((None, None, C_HEADS, page), pidx(i)) for i in range(npp)]
    grid_spec = pltpu.PrefetchScalarGridSpec(
        num_scalar_prefetch=1,
        grid=(nb, npg // npp),
        in_specs=in_specs,
        out_specs=pl.BlockSpec((t_new, D), lambda b, g, pt: (b, 0)),
        scratch_shapes=[pltpu.VMEM((nrow, D), BF16), pltpu.VMEM((nrow, 1), F32), pltpu.VMEM((nrow, 1), F32),
                        pltpu.VMEM((nrow, 1), F32), pltpu.VMEM((nrow, D), F32), pltpu.VMEM((C_HEADS, 1), F32)])
    return pl.pallas_call(
        functools.partial(_foxs_kernel, t_new=t_new, npp=npp),
        grid_spec=grid_spec,
        out_shape=jax.ShapeDtypeStruct((n, D), F32),
        compiler_params=_cp("arbitrary", "arbitrary"),
        name="fox_sample",
    )(page_table.reshape(-1), qkv, qkv, qkv, logf, *([kt] * npp), *([vt] * npp), *([clft] * npp))


def kernel(x_prompt, x_sample, c_prompt, c_sample, state_rwkv, state_rwkv_shift, state_retention, cache_fox_k, cache_fox_v, cache_fox_logf, page_table, w_ada, b_ada, norm_mix_g, norm_ffn_g, final_g, w_in_even, w_out_even, rw_mu, rw_w0, rw_w2, rw_a0, rw_a2, rw_g2, rw_kk, rw_ka, rw_rk, rw_ln_g, rw_ln_b, rw_v0, rw_v1, rw_v2, ret_gn_g, w_in_odd, b_forget, w_out_odd, w_router, b_router, w_gu, b_gu, w_down, b_down):
    bp, seq, _ = x_prompt.shape
    bs, tn_, _ = x_sample.shape
    n_p, n_s = bp * seq, bs * tn_
    npt, nst = n_p // TM, n_s // TM
    depth = w_ada.shape[0]
    n_phys, page = cache_fox_k.shape[1], cache_fox_k.shape[2]
    npg = page_table.shape[1]
    past_len = npg * page
    n_odd = cache_fox_k.shape[0]
    kt_all = jnp.transpose(cache_fox_k, (0, 1, 3, 4, 2)).reshape(n_odd, n_phys, D, page)
    vt_all = jnp.transpose(cache_fox_v, (0, 1, 3, 4, 2)).reshape(n_odd, n_phys, D, page)
    clft_all = jnp.transpose(cache_fox_logf, (0, 1, 3, 2))

    x = jnp.concatenate([x_prompt.reshape(n_p, D), x_sample.reshape(n_s, D)], axis=0)
    ada = _ada(jnp.concatenate([c_prompt, c_sample], axis=0), w_ada, b_ada)

    zero_shift = jnp.zeros((bp, RWKV_PROJ), F32)
    zero_rw = jnp.zeros((bp, A_HEADS, A_HD, A_HD), F32)
    zero_ret = jnp.zeros((bp, B_HEADS, B_HD, B_HD), F32)
    new = {k: [] for k in ('rw_S_p', 'rw_S_s', 'sh_p', 'sh_s', 'ret_p', 'ret_s',
                           'k_p', 'k_s', 'v_p', 'v_s', 'lf_p', 'lf_s')}
    vf_p = vf_s = None
    for i in range(depth):
        modp = ada[i, :, :bp].reshape(6, bp, 1, D)
        mods = jnp.repeat(ada[i, :, bp:], tn_, axis=1)
        both = lambda w, tn, extra=None: (
            _inproj(x, norm_mix_g[i], modp, False, 0, npt, seq, w, tn, extra),
            _inproj(x, norm_mix_g[i], mods, True, npt, nst, seq, w, tn, extra))
        if i % 2 == 0:
            e = i // 2
            w_in = w_in_even[e]
            w_pad = jnp.concatenate([w_in[:, :RWKV_PROJ], jnp.zeros((D, 2048 - RWKV_PROJ), F32),
                                     w_in[:, RWKV_PROJ:]], axis=1).astype(BF16)
            (proj_p,), (proj_s,) = both(w_pad, 2048)
            lp = {'mu': rw_mu[e], 'w0': rw_w0[e], 'w2': rw_w2[e], 'a0': rw_a0[e], 'a2': rw_a2[e],
                  'g2': rw_g2[e], 'k_k': rw_kk[e], 'k_a': rw_ka[e], 'r_k': rw_rk[e],
                  'ln_g': rw_ln_g[e], 'ln_b': rw_ln_b[e]}
            vres = None if e == 0 else (rw_v0[e - 1], rw_v1[e - 1], rw_v2[e - 1])
            oa_p, s_p, sh_p, vf1_p = _rwkv(proj_p, bp, seq, RW_NSEQ, RW_TC, zero_shift, zero_rw, vf_p, lp, vres)
            oa_s, s_s, sh_s, vf1_s = _rwkv(proj_s, bs, tn_, RW_NSEQ, tn_, state_rwkv_shift[e], state_rwkv[e],
                                           vf_s, lp, vres)
            if e == 0:
                vf_p, vf_s = vf1_p, vf1_s
            ob_p, r_p = _retention(proj_p, bp, seq, 1, RET_L, 0, zero_ret, ret_gn_g[e])
            ob_s, r_s = _retention(proj_s, bs, tn_, 8, tn_, past_len, state_retention[e], ret_gn_g[e])
            new['rw_S_p'].append(s_p)
            new['rw_S_s'].append(s_s)
            new['sh_p'].append(sh_p)
            new['sh_s'].append(sh_s)
            new['ret_p'].append(r_p)
            new['ret_s'].append(r_s)
            w_out = w_out_even[e].astype(BF16)
            mixes_p, mixes_s, ws = [oa_p, ob_p], [oa_s, ob_s], [w_out[:A_W], w_out[A_W:]]
        else:
            j = i // 2
            w_in = w_in_odd[j]
            we = jnp.pad(w_in[:, 3 * D:], ((0, 0), (0, LANES - C_HEADS))).astype(BF16)
            be = jnp.pad(b_forget[j], (0, LANES - C_HEADS)).reshape(1, LANES)
            (qkv_p, lf_p), (qkv_s, lf_s) = both(w_in[:, :3 * D].astype(BF16), D, (we, be))
            c = _cumsum_prompt(lf_p, bp, seq)
            ct = jnp.transpose(c[:, :C_HEADS].reshape(bp, seq, C_HEADS), (0, 2, 1))
            o_p = _fox_prompt(qkv_p, c, ct, bp, seq)
            o_s = _fox_sample(qkv_s, lf_s, bs, tn_, page_table, j, kt_all, vt_all, clft_all)
            new['k_p'].append(qkv_p[1].reshape(bp, seq, C_HEADS, C_HD))
            new['k_s'].append(qkv_s[1].reshape(bs, tn_, C_HEADS, C_HD))
            new['v_p'].append(qkv_p[2].reshape(bp, seq, C_HEADS, C_HD))
            new['v_s'].append(qkv_s[2].reshape(bs, tn_, C_HEADS, C_HD))
            new['lf_p'].append(lf_p[:, :C_HEADS].reshape(bp, seq, C_HEADS))
            new['lf_s'].append(lf_s[:, :C_HEADS].reshape(bs, tn_, C_HEADS))
            mixes_p, mixes_s, ws = [o_p], [o_s], [w_out_odd[j].astype(BF16)]
        wr = jnp.pad(w_router[i], ((0, 0), (0, LANES - N_EXPERTS)))
        br = jnp.concatenate([b_router[i], jnp.full((LANES - N_EXPERTS,), NEG, F32)]).reshape(1, LANES)
        x, h, gates, idx, pos, cnt = _outproj(x, mixes_p, mixes_s, ws, modp, mods, norm_ffn_g[i], wr, br, seq)
        counts = cnt[0, :N_EXPERTS].astype(jnp.int32)
        slot_tok, block_exp, nused, slot_of = _moe_meta(idx[:, :TOP_K], pos[:, :TOP_K], counts, MOE_TM)
        y_slots = _moe_experts(h, slot_tok, block_exp, nused, i, w_gu, b_gu, w_down, b_down)
        x = _combine(x, y_slots, slot_of, gates, modp, mods, final_g, seq, i == depth - 1)

    st = lambda name: jnp.stack(new[name])
    return (x[:n_p].reshape(bp, seq, D), x[n_p:].reshape(bs, tn_, D),
            st('rw_S_p'), st('rw_S_s'), st('sh_p'), st('sh_s'), st('ret_p'), st('ret_s'),
            st('k_p'), st('k_s'), st('v_p'), st('v_s'), st('lf_p'), st('lf_s'))
```

```python
import functools
import math

import numpy as np
import jax
import jax.numpy as jnp
from jax import lax
from jax.experimental import pallas as pl
from jax.experimental.pallas import tpu as pltpu

F32 = jnp.float32
BF16 = jnp.bfloat16
HI = lax.Precision.HIGHEST

D = 1024
NORM_EPS = 1e-6
A_W = 512
A_HEADS = 8
A_HD = 64
RWKV_PROJ = 1792
LNX_EPS = 64e-5
B_HEADS = 4
B_HD = 128
GN_EPS = 1e-5
C_HEADS = 16
C_HD = 64
N_EXPERTS = 32
TOP_K = 4
SWIGLU_LIMIT = 7.0
SWIGLU_ALPHA = 1.702

V7X_VMEM_BYTES = 64 * 1024 * 1024
VMEM_LIMIT = V7X_VMEM_BYTES - 8 * 1024 * 1024
LANES = 128
NEG = -1e30

TM = 512
MOE_TM = 512
MOE_LOOKAHEAD = 2
CMB_TM = 256
CMB_UNROLL = 8
RW_NSEQ = 8
RW_GROUPS = 2
RW_TC = 64
RET_L = 256
FOX_TQ = 512
FOX_PAGES = 8


def _cp(*sem):
    return pltpu.CompilerParams(dimension_semantics=sem, vmem_limit_bytes=VMEM_LIMIT)


def _sigmoid(x):
    return 1.0 / (1.0 + jnp.exp(-x))


def _softplus(x):
    return jnp.maximum(x, 0.0) + jnp.log(1.0 + jnp.exp(-jnp.abs(x)))


def _bdot(a, b):
    return jnp.dot(a.astype(BF16), b.astype(BF16), preferred_element_type=F32)


def _hdot(a, b):
    return jnp.dot(a, b, precision=HI, preferred_element_type=F32)


def _nt_dot(a, b):
    return lax.dot_general(a, b, (((1,), (1,)), ((), ())), preferred_element_type=F32)


def _rms(x, g):
    return x * lax.rsqrt(jnp.mean(x * x, axis=-1, keepdims=True) + NORM_EPS) * g


def _store_tiles(ref, x):
    for s in range(x.shape[1] // LANES):
        ref[:, s, :] = x[:, s * LANES:(s + 1) * LANES]


def _load_tiles(ref):
    return jnp.concatenate([ref[:, s, :] for s in range(ref.shape[1])], axis=1)


def _ada_kernel(c_ref, w_ref, b_ref, o_ref):
    c = c_ref[...]
    o_ref[...] = _bdot(c * _sigmoid(c), w_ref[...]) + b_ref[...]


def _ada(c_all, w_ada, b_ada):
    nl = w_ada.shape[0]
    m = c_all.shape[0]
    return pl.pallas_call(
        _ada_kernel,
        grid=(nl, 6),
        in_specs=[pl.BlockSpec((m, D), lambda l, j: (0, 0)),
                  pl.BlockSpec((None, D, D), lambda l, j: (l, 0, j)),
                  pl.BlockSpec((None, None, 1, D), lambda l, j: (l, j, 0, 0))],
        out_specs=pl.BlockSpec((None, None, m, D), lambda l, j: (l, j, 0, 0)),
        out_shape=jax.ShapeDtypeStruct((nl, 6, m, D), F32),
        compiler_params=_cp("arbitrary", "arbitrary"),
        name="ada",
    )(c_all, w_ada, b_ada.reshape(nl, 6, 1, D))


def _modp_spec(which, tm, seq, nb):
    tps = seq // tm
    return pl.BlockSpec((None, None, 1, D), lambda i, *_: (which, jnp.minimum(i // tps, nb - 1), 0, 0))


def _mods_spec(which, tm, npt):
    return pl.BlockSpec((None, tm, D), lambda i, *_: (which, jnp.maximum(i - npt, 0), 0))


def _inproj_kernel(x_ref, g_ref, sh, sc, w_ref, *rest, has_extra):
    if has_extra:
        we_ref, be_ref, o_ref, oe_ref, h_scr = rest
    else:
        o_ref, h_scr = rest

    @pl.when(pl.program_id(1) == 0)
    def _():
        h = (_rms(x_ref[...], g_ref[...]) * (1.0 + sc[...]) + sh[...]).astype(BF16)
        h_scr[...] = h
        if has_extra:
            f = jnp.dot(h, we_ref[...], preferred_element_type=F32) + be_ref[...]
            oe_ref[...] = jnp.minimum(f, 0.0) - jnp.log(1.0 + jnp.exp(-jnp.abs(f)))

    o_ref[...] = jnp.dot(h_scr[...], w_ref[...], preferred_element_type=F32)


def _inproj(x, g, mod, per_token, tile0, ntiles, seq, w, tn, extra=None):
    nj = w.shape[1] // tn
    rows = ntiles * TM
    has_extra = extra is not None
    if per_token:
        mspec = lambda which: pl.BlockSpec((None, TM, D), lambda i, j: (which, i, 0))
    else:
        tps = seq // TM
        mspec = lambda which: pl.BlockSpec((None, None, 1, D), lambda i, j: (which, i // tps, 0, 0))
    in_specs = [pl.BlockSpec((TM, D), lambda i, j: (tile0 + i, 0)),
                pl.BlockSpec((1, D), lambda i, j: (0, 0)),
                mspec(0), mspec(1),
                pl.BlockSpec((D, tn), lambda i, j: (0, j))]
    args = [x, g.reshape(1, D), mod, mod, w]
    out_specs = [pl.BlockSpec((None, TM, tn), lambda i, j: (j, i, 0))]
    out_shape = [jax.ShapeDtypeStruct((nj, rows, tn), F32)]
    if has_extra:
        we, be = extra
        in_specs += [pl.BlockSpec((D, LANES), lambda i, j: (0, 0)),
                     pl.BlockSpec((1, LANES), lambda i, j: (0, 0))]
        args += [we, be]
        out_specs.append(pl.BlockSpec((TM, LANES), lambda i, j: (i, 0)))
        out_shape.append(jax.ShapeDtypeStruct((rows, LANES), F32))
    return pl.pallas_call(
        functools.partial(_inproj_kernel, has_extra=has_extra),
        grid=(ntiles, nj),
        in_specs=in_specs, out_specs=out_specs, out_shape=out_shape,
        scratch_shapes=[pltpu.VMEM((TM, D), BF16)],
        compiler_params=_cp("arbitrary", "arbitrary"),
        name="inproj",
    )(*args)


def _outproj_kernel(*refs, nparts, npt):
    x_ref = refs[0]
    mixp = refs[1:1 + nparts]
    mixs = refs[1 + nparts:1 + 2 * nparts]
    ws = refs[1 + 2 * nparts:1 + 3 * nparts]
    (gp, gs, nfg, shp, scp, shs, scs, wr, br, xo, ho, go, io, po, co, cnt_s) = refs[1 + 3 * nparts:]
    i = pl.program_id(0)
    isp = i < npt

    @pl.when(i == 0)
    def _():
        cnt_s[...] = jnp.zeros(cnt_s.shape, F32)

    acc = None
    for mp, ms, w in zip(mixp, mixs, ws):
        part = _bdot(jnp.where(isp, mp[...], ms[...]), w[...])
        acc = part if acc is None else acc + part
    xn = x_ref[...] + jnp.where(isp, gp[...], gs[...]) * acc
    xo[...] = xn
    y = _rms(xn, nfg[...])
    h = y * (1.0 + jnp.where(isp, scp[...], scs[...])) + jnp.where(isp, shp[...], shs[...])
    _store_tiles(ho, h)
    logits = _hdot(h, wr[...]) + br[...]
    tm = logits.shape[0]
    lane = lax.broadcasted_iota(jnp.int32, logits.shape, 1)
    l = logits
    vals, idxs = [], []
    for _ in range(TOP_K):
        m = jnp.max(l, axis=-1, keepdims=True)
        ix = jnp.min(jnp.where(l == m, lane, LANES), axis=-1, keepdims=True)
        vals.append(m)
        idxs.append(ix)
        l = jnp.where(lane == ix, -jnp.inf, l)
    es = [jnp.exp(v - vals[0]) for v in vals]
    den = es[0] + es[1] + es[2] + es[3]
    dense = jnp.zeros(logits.shape, F32)
    for k in range(TOP_K):
        dense = jnp.where(lane == idxs[k], 1.0, dense)
    before = (lax.broadcasted_iota(jnp.int32, (tm, tm), 0) > lax.broadcasted_iota(jnp.int32, (tm, tm), 1))
    rank = jnp.dot(jnp.where(before, 1.0, 0.0).astype(BF16), dense.astype(BF16),
                   preferred_element_type=F32) + cnt_s[...]
    cnt = cnt_s[...] + jnp.sum(dense, axis=0, keepdims=True)
    cnt_s[...] = cnt
    co[...] = cnt
    gates = jnp.zeros(logits.shape, F32)
    idxo = jnp.zeros(logits.shape, jnp.int32)
    poso = jnp.zeros(logits.shape, jnp.int32)
    for k in range(TOP_K):
        pk = jnp.sum(jnp.where(lane == idxs[k], rank, 0.0), axis=-1, keepdims=True)
        gates = jnp.where(lane == k, es[k] / den, gates)
        idxo = jnp.where(lane == k, idxs[k], idxo)
        poso = jnp.where(lane == k, pk.astype(jnp.int32), poso)
    go[...] = gates
    io[...] = idxo
    po[...] = poso


def _outproj(x, mixes_p, mixes_s, ws, modp, mods, nfg, wr, br, seq):
    n = x.shape[0]
    n_s = mods.shape[1]
    npt = (n - n_s) // TM
    nb = modp.shape[1]
    nparts = len(ws)
    row = lambda i: (i, 0)
    const = lambda i: (0, 0)
    in_specs = [pl.BlockSpec((TM, D), row)]
    in_specs += [pl.BlockSpec((TM, m.shape[1]), lambda i: (jnp.minimum(i, npt - 1), 0)) for m in mixes_p]
    in_specs += [pl.BlockSpec((TM, m.shape[1]), lambda i: (jnp.maximum(i - npt, 0), 0)) for m in mixes_s]
    in_specs += [pl.BlockSpec(w.shape, const) for w in ws]
    in_specs += [_modp_spec(2, TM, seq, nb), _mods_spec(2, TM, npt),
                 pl.BlockSpec((1, D), const),
                 _modp_spec(3, TM, seq, nb), _modp_spec(4, TM, seq, nb),
                 _mods_spec(3, TM, npt), _mods_spec(4, TM, npt),
                 pl.BlockSpec((D, LANES), const), pl.BlockSpec((1, LANES), const)]
    args = [x, *mixes_p, *mixes_s, *ws, modp, mods, nfg.reshape(1, D), modp, modp, mods, mods, wr, br]
    return pl.pallas_call(
        functools.partial(_outproj_kernel, nparts=nparts, npt=npt),
        grid=(n // TM,),
        in_specs=in_specs,
        out_specs=[pl.BlockSpec((TM, D), row), pl.BlockSpec((TM, D // LANES, LANES), lambda i: (i, 0, 0)),
                   pl.BlockSpec((TM, LANES), row), pl.BlockSpec((TM, LANES), row),
                   pl.BlockSpec((TM, LANES), row), pl.BlockSpec((1, LANES), const)],
        out_shape=[jax.ShapeDtypeStruct((n, D), F32), jax.ShapeDtypeStruct((n, D // LANES, LANES), F32),
                   jax.ShapeDtypeStruct((n, LANES), F32), jax.ShapeDtypeStruct((n, LANES), jnp.int32),
                   jax.ShapeDtypeStruct((n, LANES), jnp.int32), jax.ShapeDtypeStruct((1, LANES), F32)],
        scratch_shapes=[pltpu.VMEM((1, LANES), F32)],
        compiler_params=_cp("arbitrary"),
        name="outproj",
    )(*args)


def _moe_kernel(tok_ref, bexp_ref, nused_ref, h_hbm, wgu_ref, bgu_ref, wd_ref, bd_ref,
                o_ref, *rest, tm):
    nbuf = MOE_LOOKAHEAD + 1
    bufs = rest[:nbuf]
    wgu_s, wd_s, sem = rest[nbuf:]
    blk = pl.program_id(0)
    nblk = pl.num_programs(0)
    nu = nused_ref[0]

    def row_copy(b, r, buf, s):
        return pltpu.make_async_copy(h_hbm.at[pl.ds(tok_ref[b * tm + r], 1)], buf.at[pl.ds(r, 1)], sem.at[s])

    def issue_loop(b, buf, s):
        def body(r, c):
            row_copy(b, r, buf, s).start()
            return c
        lax.fori_loop(0, tm, body, 0)

    @pl.when(blk == 0)
    def _():
        for b in range(MOE_LOOKAHEAD):
            issue_loop(b, bufs[b], b)

    def step(cur, cs, nxt, ns):
        pltpu.make_async_copy(h_hbm.at[pl.ds(0, tm)], cur, sem.at[cs]).wait()

        @pl.when(blk < nu)
        def _():
            e = bexp_ref[blk]
            e_prev = bexp_ref[jnp.maximum(blk - 1, 0)]

            @pl.when((blk == 0) | (e != e_prev))
            def _():
                wgu_s[...] = wgu_ref[...].astype(BF16)
                wd_s[...] = wd_ref[...].astype(BF16)

            for r in range(tm):
                row_copy(blk + MOE_LOOKAHEAD, r, nxt, ns).start()
            xb = _load_tiles(cur).astype(BF16)
            gu = jnp.dot(xb, wgu_s[...], preferred_element_type=F32) + bgu_ref[...]
            gate = jnp.minimum(gu[:, :D], SWIGLU_LIMIT)
            up = jnp.clip(gu[:, D:], -SWIGLU_LIMIT, SWIGLU_LIMIT)
            act = (up + 1.0) * gate * _sigmoid(SWIGLU_ALPHA * gate)
            o_ref[...] = jnp.dot(act.astype(BF16), wd_s[...], preferred_element_type=F32) + bd_ref[...]

        @pl.when(blk >= nu)
        def _():
            @pl.when(blk + MOE_LOOKAHEAD < nblk)
            def _():
                issue_loop(blk + MOE_LOOKAHEAD, nxt, ns)
            o_ref[...] = jnp.zeros(o_ref.shape, F32)

    for k in range(nbuf):
        ahead = (k + MOE_LOOKAHEAD) % nbuf
        pl.when(blk % nbuf == k)(functools.partial(step, bufs[k], k, bufs[ahead], ahead))


def _moe_experts(h, slot_tok, block_exp, nused, layer, w_gu, b_gu, w_down, b_down):
    tm = MOE_TM
    nb = block_exp.shape[0]
    nl = w_gu.shape[0]
    grid_spec = pltpu.PrefetchScalarGridSpec(
        num_scalar_prefetch=3,
        grid=(nb,),
        in_specs=[pl.BlockSpec(memory_space=pl.ANY),
                  pl.BlockSpec((None, None, D, 2 * D), lambda b, t, e, u: (layer, e[b], 0, 0)),
                  pl.BlockSpec((None, None, 1, 2 * D), lambda b, t, e, u: (layer, e[b], 0, 0)),
                  pl.BlockSpec((None, None, D, D), lambda b, t, e, u: (layer, e[b], 0, 0)),
                  pl.BlockSpec((None, None, 1, D), lambda b, t, e, u: (layer, e[b], 0, 0))],
        out_specs=pl.BlockSpec((tm, D), lambda b, t, e, u: (b, 0)),
        scratch_shapes=[pltpu.VMEM((tm, D // LANES, LANES), F32) for _ in range(MOE_LOOKAHEAD + 1)]
        + [pltpu.VMEM((D, 2 * D), BF16),
           pltpu.VMEM((D, D), BF16),
           pltpu.SemaphoreType.DMA((MOE_LOOKAHEAD + 1,))])
    return pl.pallas_call(
        functools.partial(_moe_kernel, tm=tm),
        grid_spec=grid_spec,
        out_shape=jax.ShapeDtypeStruct((nb * tm, D), F32),
        compiler_params=_cp("arbitrary"),
        name="moe_experts",
    )(slot_tok, block_exp, nused, h, w_gu,
      b_gu.reshape(nl, N_EXPERTS, 1, 2 * D), w_down, b_down.reshape(nl, N_EXPERTS, 1, D))


def _combine_kernel(sof_ref, y_hbm, x_ref, gt_ref, gp, gs, fg_ref, o_ref, y0, y1, sem, *, tm, npt, final):
    i = pl.program_id(0)
    nt = pl.num_programs(0)

    def issue(t, buf, s, unroll):
        base = t * (tm * TOP_K)

        def body(r, c):
            for k in range(TOP_K):
                slot = sof_ref[base + r * TOP_K + k]
                pltpu.make_async_copy(y_hbm.at[pl.ds(slot, 1)], buf.at[k, pl.ds(r, 1)], sem.at[s]).start()
            return c

        lax.fori_loop(0, tm, body, 0, unroll=unroll)

    @pl.when(i == 0)
    def _():
        issue(0, y0, 0, 4)

    def step(cur, cs, nxt, ns):
        @pl.when(i + 1 < nt)
        def _():
            issue(i + 1, nxt, ns, CMB_UNROLL)

        for k in range(TOP_K):
            pltpu.make_async_copy(y_hbm.at[pl.ds(0, tm)], cur.at[k], sem.at[cs]).wait()
        gt = gt_ref[...]
        y = ((cur[0] * gt[:, 0:1] + cur[1] * gt[:, 1:2]) + (cur[2] * gt[:, 2:3] + cur[3] * gt[:, 3:4]))
        xn = x_ref[...] + jnp.where(i < npt, gp[...], gs[...]) * y
        if final:
            xn = _rms(xn, fg_ref[...])
        o_ref[...] = xn

    pl.when(i % 2 == 0)(functools.partial(step, y0, 0, y1, 1))
    pl.when(i % 2 == 1)(functools.partial(step, y1, 1, y0, 0))


def _combine(x, y_slots, slot_of, gates, modp, mods, final_g, seq, final):
    tm = CMB_TM
    n = x.shape[0]
    n_s = mods.shape[1]
    npt = (n - n_s) // tm
    nb = modp.shape[1]
    grid_spec = pltpu.PrefetchScalarGridSpec(
        num_scalar_prefetch=1,
        grid=(n // tm,),
        in_specs=[pl.BlockSpec(memory_space=pl.ANY),
                  pl.BlockSpec((tm, D), lambda i, s: (i, 0)),
                  pl.BlockSpec((tm, LANES), lambda i, s: (i, 0)),
                  _modp_spec(5, tm, seq, nb), _mods_spec(5, tm, npt),
                  pl.BlockSpec((1, D), lambda i, s: (0, 0))],
        out_specs=pl.BlockSpec((tm, D), lambda i, s: (i, 0)),
        scratch_shapes=[pltpu.VMEM((TOP_K, tm, D), F32), pltpu.VMEM((TOP_K, tm, D), F32),
                        pltpu.SemaphoreType.DMA((2,))])
    return pl.pallas_call(
        functools.partial(_combine_kernel, tm=tm, npt=npt, final=final),
        grid_spec=grid_spec,
        out_shape=jax.ShapeDtypeStruct((n, D), F32),
        compiler_params=_cp("arbitrary"),
        name="moe_combine",
    )(slot_of, y_slots, x, gates, modp, mods, final_g.reshape(1, D))


def _moe_meta(idx4, pos4, counts, tm):
    n = idx4.shape[0]
    nk = n * TOP_K
    nb = -(-(nk + N_EXPERTS * (tm - 1)) // tm) + MOE_LOOKAHEAD
    padded = (counts + tm - 1) // tm * tm
    pad_end = jnp.cumsum(padded)
    pad_start = pad_end - padded
    flat = (pad_start[idx4] + pos4).astype(jnp.int32).reshape(nk)
    tok = jnp.repeat(jnp.arange(n, dtype=jnp.int32), TOP_K)
    slot_tok = jnp.zeros((nb * tm,), jnp.int32).at[flat].set(tok)
    starts = jnp.arange(nb, dtype=jnp.int32) * tm
    block_exp = jnp.minimum(jnp.sum((pad_end[None, :] <= starts[:, None]).astype(jnp.int32), axis=1),
                            N_EXPERTS - 1).astype(jnp.int32)
    nused = (pad_end[-1] // tm).astype(jnp.int32).reshape(1)
    return slot_tok, block_exp, nused, flat


def _seg_consts():
    r = np.arange(256)
    mseg = (r[:, None] // A_HD == r[None, :] // A_HD).astype(np.float32)
    c = np.arange(A_W)
    eyet = (np.arange(A_HD)[:, None] == (c[None, :] % A_HD)).astype(np.float32)
    return jnp.asarray(mseg, BF16), jnp.asarray(eyet, F32)


def _split(x):
    hi = x.astype(BF16)
    lo = (x - hi.astype(F32)).astype(BF16)
    return hi, lo


def _segsum(x, mseg):
    hi, lo = _split(x)
    halves = []
    for hf in range(2):
        sl = slice(hf * 256, (hf + 1) * 256)
        halves.append(jnp.dot(hi[:, sl], mseg, preferred_element_type=F32)
                      + jnp.dot(lo[:, sl], mseg, preferred_element_type=F32))
    return jnp.concatenate(halves, axis=1)


def _rwkv_kernel(*refs, nseq, tc, has_vres, ngrp):
    it = iter(refs)
    p_ref, st_ref, s0_ref = next(it), next(it), next(it)
    vf_ref = next(it) if has_vres else None
    mu, w0, w2, a0, a2, g2, kkp, kap, rk, lng, lnb = (next(it) for _ in range(11))
    if has_vres:
        v0, v1, v2 = next(it), next(it), next(it)
    mseg_ref, eyet_ref = next(it), next(it)
    o_ref, sout_ref, shout_ref = next(it), next(it), next(it)
    vfo_ref = None if has_vres else next(it)
    w_s, k_s, v_s, nkk_s, b_s, r_s, out_s, st_s, prev_s = (next(it) for _ in range(9))

    c = pl.program_id(1)
    nch = pl.num_programs(1)
    rows = nseq * tc
    mseg = mseg_ref[...]
    mseg2 = jnp.concatenate([mseg, mseg], axis=0)
    eyet = eyet_ref[...]

    @pl.when(c == 0)
    def _():
        prev_s[...] = st_ref[...]
        for s in range(nseq):
            for h in range(A_HEADS):
                st_s[s, :, h * A_HD:(h + 1) * A_HD] = s0_ref[s, h]

    p = p_ref[:, :, :RWKV_PROJ].reshape(rows, RWKV_PROJ)
    rolled = pltpu.roll(p, 1, 0)
    ridx = lax.broadcasted_iota(jnp.int32, (rows, 1), 0)
    prev = rolled
    for s in range(nseq):
        prev = jnp.where(ridx == s * tc, prev_s[s:s + 1, :], prev)
    z = p + (prev - p) * mu[...]
    r = z[:, 0:512]
    k = z[:, 512:1024]
    v = z[:, 1024:1536]
    wi = z[:, 1536:1600]
    ai = z[:, 1600:1664]
    gi = z[:, 1664:1792]
    w_log = -_softplus(-(w0[...] + _hdot(jnp.tanh(wi), w2[...]))) - 0.5
    w = jnp.exp(-jnp.exp(w_log))
    a = _sigmoid(a0[...] + _hdot(ai, a2[...]))
    g = _bdot(_sigmoid(gi), g2[...])
    if has_vres:
        vf = vf_ref[...].reshape(rows, A_W)
        v = v + (vf - v) * _sigmoid(v0[...] + _bdot(_bdot(v, v1[...]), v2[...]))
    else:
        vfo_ref[...] = v.reshape(nseq, tc, A_W)
    kk = k * kkp[...]
    kk = kk * lax.rsqrt(jnp.maximum(_segsum(kk * kk, mseg), 1e-24))
    k = k * (1.0 + (a - 1.0) * kap[...])
    w_s[...] = w
    k_s[...] = k
    v_s[...] = v
    nkk_s[...] = -kk
    b_s[...] = kk * a
    r_s[...] = r
    bonus = _segsum(r * k * rk[...], mseg) * v

    def halves(x):
        return [x[:, 0:256], x[:, 256:512]]

    def token(t, carry):
        groups = [range(g0, g0 + nseq // ngrp) for g0 in range(0, nseq, nseq // ngrp)]
        ress = []
        for grp in groups:
            lhs, lhv = [], []
            for s in grp:
                row = s * tc + t
                x1h, x1l = _split(st_s[s] * nkk_s[pl.ds(row, 1), :])
                lhs += [jnp.concatenate([a, b], axis=1) for a, b in zip(halves(x1h), halves(x1l))]
                lhv += halves((eyet * v_s[pl.ds(row, 1), :]).astype(BF16))
            ress.append((jnp.dot(jnp.concatenate(lhs, axis=0), mseg2, preferred_element_type=F32),
                         jnp.dot(jnp.concatenate(lhv, axis=0), mseg, preferred_element_type=F32)))
        ress2 = []
        for grp, (res, resv) in zip(groups, ress):
            lhs2 = []
            for n, s in enumerate(grp):
                row = s * tc + t
                o = n * 2 * A_HD
                sa = jnp.concatenate([res[o:o + A_HD], res[o + A_HD:o + 2 * A_HD]], axis=1)
                v2 = jnp.concatenate([resv[o:o + A_HD], resv[o + A_HD:o + 2 * A_HD]], axis=1)
                st = (st_s[s] * w_s[pl.ds(row, 1), :] + sa * b_s[pl.ds(row, 1), :]
                      + v2 * k_s[pl.ds(row, 1), :])
                st_s[s] = st
                lhs2 += halves((st * r_s[pl.ds(row, 1), :]).astype(BF16))
            ress2.append(jnp.dot(jnp.concatenate(lhs2, axis=0), mseg, preferred_element_type=F32))
        for grp, res2 in zip(groups, ress2):
            for n, s in enumerate(grp):
                row = s * tc + t
                o = n * 2 * A_HD
                o2 = jnp.concatenate([res2[o:o + A_HD], res2[o + A_HD:o + 2 * A_HD]], axis=1)
                out_s[pl.ds(row, 1), :] = jnp.sum(o2 * eyet, axis=0, keepdims=True)
        return carry

    lax.fori_loop(0, tc, token, 0)

    out = out_s[...]
    xc = out - _segsum(out, mseg) * (1.0 / A_HD)
    var = _segsum(xc * xc, mseg) * (1.0 / A_HD)
    y = xc * lax.rsqrt(var + LNX_EPS) * lng[...] + lnb[...]
    o_ref[...] = ((y + bonus) * g).reshape(nseq, tc, A_W)

    for s in range(nseq):
        prev_s[s:s + 1, :] = p[s * tc + tc - 1:s * tc + tc, :]

    @pl.when(c == nch - 1)
    def _():
        shout_ref[...] = prev_s[...]
        for s in range(nseq):
            for h in range(A_HEADS):
                sout_ref[s, h] = st_s[s, :, h * A_HD:(h + 1) * A_HD]


def _rwkv(proj, nb, seq, nseq, tc, st_shift, s0, vfirst, lp, vres):
    nch = seq // tc
    rows = nseq * tc
    has_vres = vres is not None
    blk3 = lambda b, c: (b, c, 0)
    vec = lambda a: a.reshape(1, -1)
    c2 = lambda b, c: (0, 0)
    mseg, eyet = _seg_consts()
    in_specs = [pl.BlockSpec((None, nseq, tc, 2048), lambda b, c: (0, b, c, 0)),
                pl.BlockSpec((None, nseq, RWKV_PROJ), lambda b, c: (b, 0, 0)),
                pl.BlockSpec((nseq, A_HEADS, A_HD, A_HD), lambda b, c: (b, 0, 0, 0))]
    args = [proj.reshape(proj.shape[0], nb, seq, 2048), st_shift.reshape(nb // nseq, nseq, RWKV_PROJ), s0]
    if has_vres:
        in_specs.append(pl.BlockSpec((nseq, tc, A_W), blk3))
        args.append(vfirst)
    small = [vec(lp['mu']), vec(lp['w0']), lp['w2'], vec(lp['a0']), lp['a2'], lp['g2'].astype(BF16),
             vec(lp['k_k']), vec(lp['k_a']), vec(lp['r_k']), vec(lp['ln_g']), vec(lp['ln_b'])]
    if has_vres:
        small += [vec(vres[0]), vres[1].astype(BF16), vres[2].astype(BF16)]
    small += [mseg, eyet]
    in_specs += [pl.BlockSpec(a.shape, c2) for a in small]
    args += small
    out_specs = [pl.BlockSpec((nseq, tc, A_W), blk3),
                 pl.BlockSpec((nseq, A_HEADS, A_HD, A_HD), lambda b, c: (b, 0, 0, 0)),
                 pl.BlockSpec((None, nseq, RWKV_PROJ), lambda b, c: (b, 0, 0))]
    out_shape = [jax.ShapeDtypeStruct((nb, seq, A_W), F32),
                 jax.ShapeDtypeStruct((nb, A_HEADS, A_HD, A_HD), F32),
                 jax.ShapeDtypeStruct((nb // nseq, nseq, RWKV_PROJ), F32)]
    if not has_vres:
        out_specs.append(pl.BlockSpec((nseq, tc, A_W), blk3))
        out_shape.append(jax.ShapeDtypeStruct((nb, seq, A_W), F32))
    scratch = [pltpu.VMEM((rows, A_W), F32) for _ in range(7)]
    scratch += [pltpu.VMEM((nseq, A_HD, A_W), F32), pltpu.VMEM((nseq, RWKV_PROJ), F32)]
    outs = pl.pallas_call(
        functools.partial(_rwkv_kernel, nseq=nseq, tc=tc, has_vres=has_vres, ngrp=RW_GROUPS),
        grid=(nb // nseq, nch),
        in_specs=in_specs, out_specs=out_specs, out_shape=out_shape,
        scratch_shapes=scratch,
        compiler_params=_cp("arbitrary", "arbitrary"),
        name="rwkv",
    )(*args)
    o, s_out, sh_out = outs[0].reshape(nb * seq, A_W), outs[1], outs[2].reshape(nb, RWKV_PROJ)
    vf_out = vfirst if has_vres else outs[3]
    return o, s_out, sh_out, vf_out


def _ret_kernel(pr_ref, s0_ref, cos_ref, sin_ref, gn_ref, o_ref, sout_ref, st_s, *, nseq, L):
    c = pl.program_id(1)
    nch = pl.num_programs(1)

    @pl.when(c == 0)
    def _():
        st_s[...] = s0_ref[...]

    cos = cos_ref[...]
    sin = sin_ref[...]
    ii = lax.broadcasted_iota(jnp.int32, (L, L), 0)
    jj = lax.broadcasted_iota(jnp.int32, (L, L), 1)
    dif = (ii - jj).astype(F32)
    ri = lax.broadcasted_iota(jnp.int32, (L, 1), 0).astype(F32)

    def rot(x):
        return x * cos + pltpu.roll(x, B_HD // 2, 1) * sin

    for h in range(B_HEADS):
        lg = math.log1p(-2.0 ** (-5.0 - h))
        intra = jnp.where(dif >= 0, jnp.exp(jnp.maximum(dif, 0.0) * lg), 0.0)
        q_dec = jnp.exp((ri + 1.0) * lg)
        k_dec = jnp.exp((L - 1.0 - ri) * lg)
        c_dec = math.exp(L * lg)
        gn = gn_ref[:, h * B_HD:(h + 1) * B_HD]
        for s in range(nseq):
            rs = slice(s * L, (s + 1) * L)
            q = rot(pr_ref[rs, h * B_HD:(h + 1) * B_HD])
            k = rot(pr_ref[rs, 512 + h * B_HD:512 + (h + 1) * B_HD]) * (B_HD ** -0.5)
            v = pr_ref[rs, 1024 + h * B_HD:1024 + (h + 1) * B_HD]
            g = pr_ref[rs, 1536 + h * B_HD:1536 + (h + 1) * B_HD]
            st = st_s[s, h]
            vb = v.astype(BF16)
            att = _nt_dot(q.astype(BF16), k.astype(BF16)) * intra
            o = (jnp.dot(att.astype(BF16), vb, preferred_element_type=F32)
                 + _bdot(q * q_dec, st))
            st_s[s, h] = st * c_dec + lax.dot_general(
                (k * k_dec).astype(BF16), vb, (((0,), (0,)), ((), ())), preferred_element_type=F32)
            oc = o - jnp.mean(o, axis=-1, keepdims=True)
            y = oc * lax.rsqrt(jnp.mean(oc * oc, axis=-1, keepdims=True) + GN_EPS) * gn
            o_ref[rs, h * B_HD:(h + 1) * B_HD] = g * _sigmoid(g) * y

    @pl.when(c == nch - 1)
    def _():
        sout_ref[...] = st_s[...]


def _rot_tables(pos0, t):
    half = B_HD // 2
    inv = 1.0 / (10000.0 ** jnp.linspace(0.0, 1.0, half, dtype=F32))
    ang = (pos0 + jnp.arange(t, dtype=F32))[:, None] * inv[None, :]
    cos, sin = jnp.cos(ang), jnp.sin(ang)
    return jnp.concatenate([cos, cos], axis=1), jnp.concatenate([-sin, sin], axis=1)


def _retention(proj, nb, seq, nseq, L, pos0, s0, gn_g):
    n = proj.shape[1]
    nch = seq // L
    rows = nseq * L
    cos, sin = _rot_tables(float(pos0), seq)
    rowmap = lambda b, c: (b * nch + c, 0)
    return pl.pallas_call(
        functools.partial(_ret_kernel, nseq=nseq, L=L),
        grid=(nb // nseq, nch),
        in_specs=[pl.BlockSpec((None, rows, 2048), lambda b, c: (1, b * nch + c, 0)),
                  pl.BlockSpec((nseq, B_HEADS, B_HD, B_HD), lambda b, c: (b, 0, 0, 0)),
                  pl.BlockSpec((L, B_HD), lambda b, c: (c, 0)),
                  pl.BlockSpec((L, B_HD), lambda b, c: (c, 0)),
                  pl.BlockSpec((1, 512), lambda b, c: (0, 0))],
        out_specs=[pl.BlockSpec((rows, 512), rowmap),
                   pl.BlockSpec((nseq, B_HEADS, B_HD, B_HD), lambda b, c: (b, 0, 0, 0))],
        out_shape=[jax.ShapeDtypeStruct((n, 512), F32),
                   jax.ShapeDtypeStruct((nb, B_HEADS, B_HD, B_HD), F32)],
        scratch_shapes=[pltpu.VMEM((nseq, B_HEADS, B_HD, B_HD), F32)],
        compiler_params=_cp("arbitrary", "arbitrary"),
        name="retention",
    )(proj, s0, cos, sin, gn_g.reshape(1, 512))


def _tri(n):
    return (lax.broadcasted_iota(jnp.int32, (n, n), 0) >= lax.broadcasted_iota(jnp.int32, (n, n), 1)).astype(F32)


def _cumsum_kernel(lf_ref, o_ref):
    t = lf_ref.shape[0]
    tri = _tri(LANES)
    carry = jnp.zeros((1, LANES), F32)
    for b in range(t // LANES):
        rs = slice(b * LANES, (b + 1) * LANES)
        cb = _hdot(tri, lf_ref[rs, :]) + carry
        o_ref[rs, :] = cb
        carry = cb[LANES - 1:LANES, :]


def _cumsum_prompt(logf, nb, seq):
    return pl.pallas_call(
        _cumsum_kernel,
        grid=(nb,),
        in_specs=[pl.BlockSpec((seq, LANES), lambda b: (b, 0))],
        out_specs=pl.BlockSpec((seq, LANES), lambda b: (b, 0)),
        out_shape=jax.ShapeDtypeStruct((nb * seq, LANES), F32),
        compiler_params=_cp("arbitrary"),
        name="fox_cumsum",
    )(logf)


def _foxp_kernel(q_ref, k_ref, v_ref, cq_ref, ck_ref, o_ref, *, tq):
    hp = pl.program_id(1)
    qi = pl.program_id(2)
    lane = lax.broadcasted_iota(jnp.int32, (tq, LANES), 1)
    qs, cqs = [], []
    for hh in range(2):
        hs = slice(hh * C_HD, (hh + 1) * C_HD)
        qs.append((q_ref[:, hs] * (C_HD ** -0.5)).astype(BF16))
        cqs.append(jnp.sum(jnp.where(lane == hp * 2 + hh, cq_ref[...], 0.0), axis=-1, keepdims=True))

    def block(off, diag, carry):
        new = []
        for hh in range(2):
            m, l, acc = carry[3 * hh:3 * hh + 3]
            hs = slice(hh * C_HD, (hh + 1) * C_HD)
            k = k_ref[pl.ds(off, tq), hs].astype(BF16)
            v = v_ref[pl.ds(off, tq), hs].astype(BF16)
            s = _nt_dot(qs[hh], k) + (cqs[hh] - ck_ref[pl.ds(hp * 2 + hh, 1), pl.ds(off, tq)])
            if diag:
                s = jnp.where(lax.broadcasted_iota(jnp.int32, (tq, tq), 0)
                              >= lax.broadcasted_iota(jnp.int32, (tq, tq), 1), s, NEG)
            m_new = jnp.maximum(m, jnp.max(s, axis=-1, keepdims=True))
            alpha = jnp.exp(m - m_new)
            p = jnp.exp(s - m_new)
            new += [m_new, alpha * l + jnp.sum(p, axis=-1, keepdims=True),
                    alpha * acc + jnp.dot(p.astype(BF16), v, preferred_element_type=F32)]
        return tuple(new)

    init = (jnp.full((tq, 1), NEG, F32), jnp.zeros((tq, 1), F32), jnp.zeros((tq, C_HD), F32)) * 2
    carry = lax.fori_loop(0, qi, lambda kj, c: block(pl.multiple_of(kj * tq, tq), False, c), init)
    carry = block(pl.multiple_of(qi * tq, tq), True, carry)
    o_ref[...] = jnp.concatenate([carry[2] / carry[1], carry[5] / carry[4]], axis=1)


def _fox_prompt(qkv, c, ct, nb, seq):
    n = qkv.shape[1]
    tq = FOX_TQ
    nq = seq // tq
    return pl.pallas_call(
        functools.partial(_foxp_kernel, tq=tq),
        grid=(nb, C_HEADS // 2, nq),
        in_specs=[pl.BlockSpec((None, tq, LANES), lambda b, hp, qi: (0, b * nq + qi, hp)),
                  pl.BlockSpec((None, seq, LANES), lambda b, hp, qi: (1, b, hp)),
                  pl.BlockSpec((None, seq, LANES), lambda b, hp, qi: (2, b, hp)),
                  pl.BlockSpec((tq, LANES), lambda b, hp, qi: (b * nq + qi, 0)),
                  pl.BlockSpec((None, C_HEADS, seq), lambda b, hp, qi: (b, 0, 0))],
        out_specs=pl.BlockSpec((tq, LANES), lambda b, hp, qi: (b * nq + qi, hp)),
        out_shape=jax.ShapeDtypeStruct((n, D), F32),
        compiler_params=_cp("arbitrary", "arbitrary", "arbitrary"),
        name="fox_prompt",
    )(qkv, qkv, qkv, c, ct)


def _foxs_kernel(pt_ref, q_ref, kn_ref, vn_ref, lfn_ref, *rest, t_new, npp):
    kts, vts, clfs = rest[0:npp], rest[npp:2 * npp], rest[2 * npp:3 * npp]
    o_ref, qbd, cnq_s, m_s, l_s, acc_s, carry = rest[3 * npp:]
    g = pl.program_id(1)
    nrow = C_HEADS * t_new
    page = clfs[0].shape[1]
    rowh = lax.broadcasted_iota(jnp.int32, (nrow, D), 0) // t_new
    bmask = (lax.broadcasted_iota(jnp.int32, (nrow, D), 1) // C_HD) == rowh

    def per_row(x):
        return jnp.concatenate([jnp.broadcast_to(x[h:h + 1, :], (t_new, x.shape[1])) for h in range(C_HEADS)],
                               axis=0)

    def update(s, pv):
        m = m_s[...]
        m_new = jnp.maximum(m, jnp.max(s, axis=-1, keepdims=True))
        alpha = jnp.exp(m - m_new)
        p = jnp.exp(s - m_new)
        l_s[...] = alpha * l_s[...] + jnp.sum(p, axis=-1, keepdims=True)
        acc_s[...] = alpha * acc_s[...] + pv(p)
        m_s[...] = m_new

    @pl.when(g == 0)
    def _():
        q = q_ref[...] * (C_HD ** -0.5)
        qbd[...] = jnp.where(bmask, jnp.concatenate([q] * C_HEADS, axis=0), 0.0).astype(BF16)
        m_s[...] = jnp.full(m_s.shape, NEG, F32)
        l_s[...] = jnp.zeros(l_s.shape, F32)
        acc_s[...] = jnp.zeros(acc_s.shape, F32)
        carry[...] = jnp.zeros(carry.shape, F32)
        tle = (lax.broadcasted_iota(jnp.int32, (t_new, t_new), 0)
               <= lax.broadcasted_iota(jnp.int32, (t_new, t_new), 1)).astype(F32)
        cnt = lax.dot_general(lfn_ref[:, :C_HEADS], tle, (((0,), (0,)), ((), ())),
                              precision=HI, preferred_element_type=F32)
        gk = per_row(cnt)
        colt = lax.broadcasted_iota(jnp.int32, (nrow, t_new), 1)
        qrow = lax.broadcasted_iota(jnp.int32, (nrow, t_new), 0) % t_new
        cnq = jnp.sum(jnp.where(colt == qrow, gk, 0.0), axis=-1, keepdims=True)
        cnq_s[...] = cnq
        s2 = _nt_dot(qbd[...], kn_ref[...].astype(BF16)) + cnq - gk
        update(jnp.where(colt <= qrow, s2, NEG),
               lambda p: jnp.dot(p, vn_ref[...], preferred_element_type=F32))

    later = (lax.broadcasted_iota(jnp.int32, (page, page), 0)
             > lax.broadcasted_iota(jnp.int32, (page, page), 1)).astype(F32)
    lfs = [r[...] for r in clfs]
    suf = _hdot(jnp.concatenate(lfs, axis=0), later)
    run = carry[...]
    ds = []
    for i in range(npp):
        si = suf[i * C_HEADS:(i + 1) * C_HEADS]
        ds.append(si + run)
        run = run + si[:, 0:1] + lfs[i][:, 0:1]
    carry[...] = run
    qb = qbd[...]
    s = jnp.concatenate([jnp.dot(qb, kt[...].astype(BF16), preferred_element_type=F32) for kt in kts], axis=1)
    s = s + (per_row(jnp.concatenate(ds, axis=1)) + cnq_s[...])

    def pv(p):
        pb = p.astype(BF16)
        out = _nt_dot(pb[:, 0:page], vts[0][...].astype(BF16))
        for i in range(1, npp):
            out = out + _nt_dot(pb[:, i * page:(i + 1) * page], vts[i][...].astype(BF16))
        return out

    update(s, pv)

    @pl.when(g == pl.num_programs(1) - 1)
    def _():
        o = jnp.where(bmask, acc_s[...] / l_s[...], 0.0)
        o_ref[...] = jnp.sum(o.reshape(C_HEADS, t_new, D), axis=0)


def _fox_sample(qkv, logf, nb, t_new, page_table, layer, kt, vt, clft):
    n = qkv.shape[1]
    npg = page_table.shape[1]
    page = kt.shape[3]
    nrow = C_HEADS * t_new
    npp = FOX_PAGES

    def pidx(i):
        return lambda b, g, pt: (layer, pt[b * npg + (npg - 1 - (g * npp + i))], 0, 0)

    in_specs = [pl.BlockSpec((None, t_new, D), lambda b, g, pt: (0, b, 0)),
                pl.BlockSpec((None, t_new, D), lambda b, g, pt: (1, b, 0)),
                pl.BlockSpec((None, t_new, D), lambda b, g, pt: (2, b, 0)),
                pl.BlockSpec((t_new, LANES), lambda b, g, pt: (b, 0))]
    in_specs += [pl.BlockSpec((None, None, D, page), pidx(i)) for i in range(npp)]
    in_specs += [pl.BlockSpec((None, None, D, page), pidx(i)) for i in range(npp)]
    in_specs += [pl.BlockSpec((None, None, C_HEADS, page), pidx(i)) for i in range(npp)]
    grid_spec = pltpu.PrefetchScalarGridSpec(
        num_scalar_prefetch=1,
        grid=(nb, npg // npp),
        in_specs=in_specs,
        out_specs=pl.BlockSpec((t_new, D), lambda b, g, pt: (b, 0)),
        scratch_shapes=[pltpu.VMEM((nrow, D), BF16), pltpu.VMEM((nrow, 1), F32), pltpu.VMEM((nrow, 1), F32),
                        pltpu.VMEM((nrow, 1), F32), pltpu.VMEM((nrow, D), F32), pltpu.VMEM((C_HEADS, 1), F32)])
    return pl.pallas_call(
        functools.partial(_foxs_kernel, t_new=t_new, npp=npp),
        grid_spec=grid_spec,
        out_shape=jax.ShapeDtypeStruct((n, D), F32),
        compiler_params=_cp("arbitrary", "arbitrary"),
        name="fox_sample",
    )(page_table.reshape(-1), qkv, qkv, qkv, logf, *([kt] * npp), *([vt] * npp), *([clft] * npp))


def kernel(x_prompt, x_sample, c_prompt, c_sample, state_rwkv, state_rwkv_shift, state_retention, cache_fox_k, cache_fox_v, cache_fox_logf, page_table, w_ada, b_ada, norm_mix_g, norm_ffn_g, final_g, w_in_even, w_out_even, rw_mu, rw_w0, rw_w2, rw_a0, rw_a2, rw_g2, rw_kk, rw_ka, rw_rk, rw_ln_g, rw_ln_b, rw_v0, rw_v1, rw_v2, ret_gn_g, w_in_odd, b_forget, w_out_odd, w_router, b_router, w_gu, b_gu, w_down, b_down):
    bp, seq, _ = x_prompt.shape
    bs, tn_, _ = x_sample.shape
    n_p, n_s = bp * seq, bs * tn_
    npt, nst = n_p // TM, n_s // TM
    depth = w_ada.shape[0]
    n_phys, page = cache_fox_k.shape[1], cache_fox_k.shape[2]
    npg = page_table.shape[1]
    past_len = npg * page
    n_odd = cache_fox_k.shape[0]
    kt_all = jnp.transpose(cache_fox_k, (0, 1, 3, 4, 2)).reshape(n_odd, n_phys, D, page)
    vt_all = jnp.transpose(cache_fox_v, (0, 1, 3, 4, 2)).reshape(n_odd, n_phys, D, page)
    clft_all = jnp.transpose(cache_fox_logf, (0, 1, 3, 2))

    x = jnp.concatenate([x_prompt.reshape(n_p, D), x_sample.reshape(n_s, D)], axis=0)
    ada = _ada(jnp.concatenate([c_prompt, c_sample], axis=0), w_ada, b_ada)

    zero_shift = jnp.zeros((bp, RWKV_PROJ), F32)
    zero_rw = jnp.zeros((bp, A_HEADS, A_HD, A_HD), F32)
    zero_ret = jnp.zeros((bp, B_HEADS, B_HD, B_HD), F32)
    new = {k: [] for k in ('rw_S_p', 'rw_S_s', 'sh_p', 'sh_s', 'ret_p', 'ret_s',
                           'k_p', 'k_s', 'v_p', 'v_s', 'lf_p', 'lf_s')}
    vf_p = vf_s = None
    for i in range(depth):
        modp = ada[i, :, :bp].reshape(6, bp, 1, D)
        mods = jnp.repeat(ada[i, :, bp:], tn_, axis=1)
        both = lambda w, tn, extra=None: (
            _inproj(x, norm_mix_g[i], modp, False, 0, npt, seq, w, tn, extra),
            _inproj(x, norm_mix_g[i], mods, True, npt, nst, seq, w, tn, extra))
        if i % 2 == 0:
            e = i // 2
            w_in = w_in_even[e]
            w_pad = jnp.concatenate([w_in[:, :RWKV_PROJ], jnp.zeros((D, 2048 - RWKV_PROJ), F32),
                                     w_in[:, RWKV_PROJ:]], axis=1).astype(BF16)
            (proj_p,), (proj_s,) = both(w_pad, 2048)
            lp = {'mu': rw_mu[e], 'w0': rw_w0[e], 'w2': rw_w2[e], 'a0': rw_a0[e], 'a2': rw_a2[e],
                  'g2': rw_g2[e], 'k_k': rw_kk[e], 'k_a': rw_ka[e], 'r_k': rw_rk[e],
                  'ln_g': rw_ln_g[e], 'ln_b': rw_ln_b[e]}
            vres = None if e == 0 else (rw_v0[e - 1], rw_v1[e - 1], rw_v2[e - 1])
            oa_p, s_p, sh_p, vf1_p = _rwkv(proj_p, bp, seq, RW_NSEQ, RW_TC, zero_shift, zero_rw, vf_p, lp, vres)
            oa_s, s_s, sh_s, vf1_s = _rwkv(proj_s, bs, tn_, RW_NSEQ, tn_, state_rwkv_shift[e], state_rwkv[e],
                                           vf_s, lp, vres)
            if e == 0:
                vf_p, vf_s = vf1_p, vf1_s
            ob_p, r_p = _retention(proj_p, bp, seq, 1, RET_L, 0, zero_ret, ret_gn_g[e])
            ob_s, r_s = _retention(proj_s, bs, tn_, 8, tn_, past_len, state_retention[e], ret_gn_g[e])
            new['rw_S_p'].append(s_p)
            new['rw_S_s'].append(s_s)
            new['sh_p'].append(sh_p)
            new['sh_s'].append(sh_s)
            new['ret_p'].append(r_p)
            new['ret_s'].append(r_s)
            w_out = w_out_even[e].astype(BF16)
            mixes_p, mixes_s, ws = [oa_p, ob_p], [oa_s, ob_s], [w_out[:A_W], w_out[A_W:]]
        else:
            j = i // 2
            w_in = w_in_odd[j]
            we = jnp.pad(w_in[:, 3 * D:], ((0, 0), (0, LANES - C_HEADS))).astype(BF16)
            be = jnp.pad(b_forget[j], (0, LANES - C_HEADS)).reshape(1, LANES)
            (qkv_p, lf_p), (qkv_s, lf_s) = both(w_in[:, :3 * D].astype(BF16), D, (we, be))
            c = _cumsum_prompt(lf_p, bp, seq)
            ct = jnp.transpose(c[:, :C_HEADS].reshape(bp, seq, C_HEADS), (0, 2, 1))
            o_p = _fox_prompt(qkv_p, c, ct, bp, seq)
            o_s = _fox_sample(qkv_s, lf_s, bs, tn_, page_table, j, kt_all, vt_all, clft_all)
            new['k_p'].append(qkv_p[1].reshape(bp, seq, C_HEADS, C_HD))
            new['k_s'].append(qkv_s[1].reshape(bs, tn_, C_HEADS, C_HD))
            new['v_p'].append(qkv_p[2].reshape(bp, seq, C_HEADS, C_HD))
            new['v_s'].append(qkv_s[2].reshape(bs, tn_, C_HEADS, C_HD))
            new['lf_p'].append(lf_p[:, :C_HEADS].reshape(bp, seq, C_HEADS))
            new['lf_s'].append(lf_s[:, :C_HEADS].reshape(bs, tn_, C_HEADS))
            mixes_p, mixes_s, ws = [o_p], [o_s], [w_out_odd[j].astype(BF16)]
        wr = jnp.pad(w_router[i], ((0, 0), (0, LANES - N_EXPERTS)))
        br = jnp.concatenate([b_router[i], jnp.full((LANES - N_EXPERTS,), NEG, F32)]).reshape(1, LANES)
        x, h, gates, idx, pos, cnt = _outproj(x, mixes_p, mixes_s, ws, modp, mods, norm_ffn_g[i], wr, br, seq)
        counts = cnt[0, :N_EXPERTS].astype(jnp.int32)
        slot_tok, block_exp, nused, slot_of = _moe_meta(idx[:, :TOP_K], pos[:, :TOP_K], counts, MOE_TM)
        y_slots = _moe_experts(h, slot_tok, block_exp, nused, i, w_gu, b_gu, w_down, b_down)
        x = _combine(x, y_slots, slot_of, gates, modp, mods, final_g, seq, i == depth - 1)

    st = lambda name: jnp.stack(new[name])
    return (x[:n_p].reshape(bp, seq, D), x[n_p:].reshape(bs, tn_, D),
            st('rw_S_p'), st('rw_S_s'), st('sh_p'), st('sh_s'), st('ret_p'), st('ret_s'),
            st('k_p'), st('k_s'), st('v_p'), st('v_s'), st('lf_p'), st('lf_s'))
```
